```python
import jax, jax.numpy as jnp
from jax import lax
import numpy as np

D_MODEL = 1024
BATCH = 2
SEQ = 8192
DEPTH = 2
DEC_BATCH = 8
DEC_SEQ = 4096
PAST_LEN = 128

HEAD_DIM = 64
ROPE_DIM = HEAD_DIM // 4
ROPE_THETA = 500000.0
EPS = 1e-6
NEG_INF = -1e30

A_PATTERNS = ((128, 1), (512, 4), (2048, 16))
A_GROUPS = len(A_PATTERNS)
A_HEADS = 8
A_WIDTH = A_HEADS * HEAD_DIM

B_Q_HEADS = 8
B_KV_HEADS = 2
B_HALF_WINDOW = 128
B_BLOCK = 128
B_WIDTH = B_Q_HEADS * HEAD_DIM

C_HEADS = 8
C_DK = 64
C_DV = 64
C_CONV = 5
C_CHUNK = 64
C_WIDTH = C_HEADS * C_DV
C_QKV = 2 * C_HEADS * C_DK + C_HEADS * C_DV

N_BRANCH = 3
FFN_HIDDEN = ((8 * D_MODEL + 3 * 256 - 1) // (3 * 256)) * 256

IN_SPLITS = (
    A_GROUPS * A_HEADS * HEAD_DIM, A_GROUPS * A_HEADS * HEAD_DIM, A_GROUPS * A_HEADS * HEAD_DIM,
    B_Q_HEADS * HEAD_DIM, B_KV_HEADS * HEAD_DIM, B_KV_HEADS * HEAD_DIM,
    C_HEADS * C_DK, C_HEADS * C_DK, C_HEADS * C_DV, C_WIDTH, 2 * C_HEADS, 2 * C_HEADS,
)
IN_WIDTH = sum(IN_SPLITS)

kernel_name = 'hybrid_dilated_window_deltanet_encoder'


def rms_norm(x, gain):
    xf = x.astype(jnp.float32)
    y = xf * lax.rsqrt(jnp.mean(xf * xf, axis=-1, keepdims=True) + EPS)
    return (y * gain.astype(jnp.float32)).astype(x.dtype)


def l2_norm(x):
    return x * lax.rsqrt(jnp.sum(x * x, axis=-1, keepdims=True) + EPS)


def rope_tables(seq_len):
    pos = jnp.arange(seq_len, dtype=jnp.float32)
    inv_freq = jnp.power(jnp.float32(ROPE_THETA), -jnp.arange(0, ROPE_DIM, 2, dtype=jnp.float32) / ROPE_DIM)
    ang = pos[:, None] * inv_freq[None, :]
    return jnp.cos(ang), jnp.sin(ang)


def apply_partial_rope(x, cos, sin):
    half = ROPE_DIM // 2
    xf = x.astype(jnp.float32)
    c, s = cos[:, None, :], sin[:, None, :]
    x1, x2 = xf[..., :half], xf[..., half:ROPE_DIM]
    out = jnp.concatenate([x1 * c - x2 * s, x2 * c + x1 * s, xf[..., ROPE_DIM:]], axis=-1)
    return out.astype(x.dtype)


def band_windows(t, blk, seq_axis):
    ax = t.ndim + seq_axis
    nb = t.shape[ax] // blk
    tb = t.reshape(t.shape[:ax] + (nb, blk) + t.shape[ax + 1:])
    pad = [(0, 0)] * tb.ndim
    pad[ax] = (1, 1)
    tp = jnp.pad(tb, pad)
    return jnp.concatenate([lax.slice_in_dim(tp, i, i + nb, axis=ax) for i in range(3)], axis=ax + 1)


def banded_attention(q, k, v, valid, half_window, blk, sink=None):
    lead = q.shape[:-3]
    L, hq, hd = q.shape[-3:]
    hk = k.shape[-2]
    rep = hq // hk
    nb = L // blk
    qb = q.reshape(lead + (nb, blk, hk, rep, hd))
    kw = band_windows(k, blk, -3)
    vw = band_windows(v, blk, -3)
    kvalid = band_windows(valid, blk, -1)
    s = jnp.einsum('...nqgrd,...nkgd->...ngrqk', qb, kw, preferred_element_type=jnp.float32) * (hd ** -0.5)
    rel = jnp.arange(blk)[:, None] - (jnp.arange(3 * blk)[None, :] - blk)
    mask = (jnp.abs(rel) <= half_window) & kvalid[..., :, None, None, None, :]
    s = jnp.where(mask, s, NEG_INF)
    m = jnp.max(s, axis=-1)
    if sink is not None:
        sink_l = sink.astype(jnp.float32).reshape(hk, rep)[:, :, None]
        m = jnp.maximum(m, sink_l)
    p = jnp.exp(s - m[..., None])
    den = jnp.sum(p, axis=-1)
    if sink is not None:
        den = den + jnp.exp(sink_l - m)
    o = jnp.einsum('...ngrqk,...nkgd->...nqgrd', p, vw.astype(jnp.float32))
    den_t = jnp.moveaxis(den, -1, -3)
    o = (o / den_t[..., None]).reshape(lead + (L, hq, hd))
    lse = (jnp.moveaxis(m, -1, -3) + jnp.log(den_t)).reshape(lead + (L, hq))
    return o.astype(q.dtype), lse


def dilated_group(q, k, v, dilation, half_span):
    B, S, H, hd = q.shape
    blk = half_span
    unit = dilation * blk
    Lp = -(-S // unit) * unit
    Ls = Lp // dilation

    def to_sub(t):
        t = jnp.pad(t, ((0, 0), (0, Lp - S), (0, 0), (0, 0)))
        return t.reshape(B, Ls, dilation, H, hd).transpose(0, 2, 1, 3, 4)

    valid = jnp.arange(Lp).reshape(Ls, dilation).T < S
    o, lse = banded_attention(to_sub(q), to_sub(k), to_sub(v), valid, half_span, blk)
    o = o.transpose(0, 2, 1, 3, 4).reshape(B, Lp, H, hd)[:, :S]
    lse = lse.transpose(0, 2, 1, 3).reshape(B, Lp, H)[:, :S]
    return o, lse


def mixer_a(q, k, v):
    B, S = q.shape[:2]
    outs, lses = [], []
    for gi, (window, dilation) in enumerate(A_PATTERNS):
        sl = slice(gi * A_HEADS, (gi + 1) * A_HEADS)
        o, lse = dilated_group(q[:, :, sl], k[:, :, sl], v[:, :, sl], dilation, (window // 2) // dilation)
        outs.append(o.astype(jnp.float32))
        lses.append(lse)
    wts = jax.nn.softmax(jnp.stack(lses, 0), axis=0)
    o = jnp.sum(wts[..., None] * jnp.stack(outs, 0), axis=0)
    return o.reshape(B, S, A_WIDTH).astype(q.dtype)


def gated_delta_chunked(q, k, v, g, beta):
    P, T, H, dk = q.shape
    dv = v.shape[-1]
    N = T // C_CHUNK

    def chunks(t):
        t = t.astype(jnp.float32).reshape((P, N, C_CHUNK, H) + t.shape[3:])
        return jnp.moveaxis(t, 3, 1)

    qc, kc, vc, gc, bc = chunks(q), chunks(k), chunks(v), chunks(g), chunks(beta)
    gcum = jnp.cumsum(gc, axis=-1)
    idx = jnp.arange(C_CHUNK)
    incl = idx[:, None] >= idx[None, :]
    strict = idx[:, None] > idx[None, :]
    decay = jnp.exp(jnp.where(incl, gcum[..., :, None] - gcum[..., None, :], NEG_INF))
    kb = kc * bc[..., None]
    vb = vc * bc[..., None]
    low = jnp.where(strict, jnp.einsum('phnid,phnjd->phnij', kb, kc) * decay, 0.0)
    a_mat = low + jnp.eye(C_CHUNK, dtype=jnp.float32)
    rhs = jnp.concatenate([vb, kb * jnp.exp(gcum)[..., None]], axis=-1)
    sol = lax.linalg.triangular_solve(a_mat, rhs, left_side=True, lower=True, unit_diagonal=True)
    u, w = sol[..., :dv], sol[..., dv:]
    qk = jnp.where(incl, jnp.einsum('phnid,phnjd->phnij', qc, kc) * decay, 0.0)
    qg = qc * jnp.exp(gcum)[..., None]
    kd = kc * jnp.exp(gcum[..., -1:] - gcum)[..., None]
    glast = jnp.exp(gcum[..., -1])

    def step(state, xs):
        u_i, w_i, qk_i, qg_i, kd_i, gl_i = xs
        v_new = u_i - jnp.einsum('phcd,phde->phce', w_i, state)
        o_i = jnp.einsum('phcd,phde->phce', qg_i, state) + jnp.einsum('phij,phje->phie', qk_i, v_new)
        state = state * gl_i[..., None, None] + jnp.einsum('phcd,phce->phde', kd_i, v_new)
        return state, o_i

    xs = tuple(jnp.moveaxis(t, 2, 0) for t in (u, w, qk, qg, kd, glast))
    state0 = jnp.zeros((P, H, dk, dv), jnp.float32)
    _, o = lax.scan(step, state0, xs)
    return jnp.transpose(o, (1, 0, 3, 2, 4)).reshape(P, T, H, dv)


def short_conv(x, w):
    ch = x.shape[-1]
    return lax.conv_general_dilated(x, w.astype(x.dtype)[:, None, :], window_strides=(1,),
                                    padding=((C_CONV // 2, C_CONV // 2),),
                                    dimension_numbers=('NWC', 'WIO', 'NWC'), feature_group_count=ch)


def mixer_c(qc, kc, vc, zc, b_raw, a_raw, conv_w, a_log, dt_bias, o_gain):
    B, S = qc.shape[:2]
    qkv = jax.nn.silu(short_conv(jnp.concatenate([qc, kc, vc], axis=-1), conv_w).astype(jnp.float32))
    q, k, v = jnp.split(qkv, [C_HEADS * C_DK, 2 * C_HEADS * C_DK], axis=-1)
    q = l2_norm(q.reshape(B, S, C_HEADS, C_DK)) * (C_DK ** -0.5)
    k = l2_norm(k.reshape(B, S, C_HEADS, C_DK))
    v = v.reshape(B, S, C_HEADS, C_DV)
    beta = jax.nn.sigmoid(b_raw.astype(jnp.float32)).reshape(B, S, 2, C_HEADS)
    g = -jnp.exp(a_log.astype(jnp.float32)) * jax.nn.softplus(
        a_raw.astype(jnp.float32).reshape(B, S, 2, C_HEADS) + dt_bias.astype(jnp.float32))
    rev = lambda t: jnp.flip(t, axis=1)
    o = gated_delta_chunked(
        jnp.concatenate([q, rev(q)], 0), jnp.concatenate([k, rev(k)], 0), jnp.concatenate([v, rev(v)], 0),
        jnp.concatenate([g[:, :, 0], rev(g[:, :, 1])], 0), jnp.concatenate([beta[:, :, 0], rev(beta[:, :, 1])], 0))
    o = o[:B] + rev(o[B:])
    o = rms_norm(o, o_gain) * jax.nn.silu(zc.astype(jnp.float32).reshape(B, S, C_HEADS, C_DV))
    return o.reshape(B, S, C_WIDTH).astype(qc.dtype)


def encoder_layer(x, norm1, w_in, qk_gain, sink, conv_w, a_log, dt_bias, o_gain, w_gate,
                  w_br_a, w_br_b, w_br_c, w_out, norm2, w_ffn_in, w_ffn_out):
    B, S, _ = x.shape
    h = rms_norm(x, norm1)
    proj = jnp.einsum('bsd,de->bse', h, w_in)
    qa, ka, va, qb, kb, vb, qc, kc, vc, zc, bc, ac = jnp.split(proj, np.cumsum(IN_SPLITS)[:-1].tolist(), axis=-1)
    cos, sin = rope_tables(S)
    ha = A_GROUPS * A_HEADS
    qa = apply_partial_rope(rms_norm(qa.reshape(B, S, ha, HEAD_DIM), qk_gain[0]), cos, sin)
    ka = apply_partial_rope(rms_norm(ka.reshape(B, S, ha, HEAD_DIM), qk_gain[1]), cos, sin)
    o_a = mixer_a(qa, ka, va.reshape(B, S, ha, HEAD_DIM))
    qb = apply_partial_rope(rms_norm(qb.reshape(B, S, B_Q_HEADS, HEAD_DIM), qk_gain[2]), cos, sin)
    kb = apply_partial_rope(rms_norm(kb.reshape(B, S, B_KV_HEADS, HEAD_DIM), qk_gain[3]), cos, sin)
    o_b, _ = banded_attention(qb, kb, vb.reshape(B, S, B_KV_HEADS, HEAD_DIM), jnp.ones((S,), dtype=bool),
                              B_HALF_WINDOW, B_BLOCK, sink)
    o_b = o_b.reshape(B, S, B_WIDTH)
    o_c = mixer_c(qc, kc, vc, zc, bc, ac, conv_w, a_log, dt_bias, o_gain)
    gates = jax.nn.sigmoid(jnp.einsum('bsd,de->bse', h, w_gate).astype(jnp.float32))
    gates = gates.reshape(B, S, N_BRANCH, D_MODEL).astype(x.dtype)
    merged = (gates[:, :, 0] * jnp.einsum('bsc,cd->bsd', o_a, w_br_a)
              + gates[:, :, 1] * jnp.einsum('bsc,cd->bsd', o_b, w_br_b)
              + gates[:, :, 2] * jnp.einsum('bsc,cd->bsd', o_c, w_br_c))
    x = x + jnp.einsum('bsd,de->bse', merged, w_out)
    h2 = rms_norm(x, norm2)
    gate, up = jnp.split(jnp.einsum('bsd,df->bsf', h2, w_ffn_in), 2, axis=-1)
    return x + jnp.einsum('bsf,fd->bsd', jax.nn.silu(gate) * up, w_ffn_out)


def setup_inputs(seed: int = 0) -> dict:
    key = jax.random.key(seed)
    ks = jax.random.split(key, 20)

    def nrm(k, shape, scale):
        return jax.random.normal(k, shape, jnp.float32) * scale

    dt = jnp.exp(jax.random.uniform(ks[9], (DEPTH, 2, C_HEADS), jnp.float32,
                                    float(np.log(1e-3)), float(np.log(1e-1))))
    return {
        'x_prompt': nrm(ks[0], (BATCH, SEQ, D_MODEL), 1.0),
        'x_sample': nrm(ks[1], (DEC_BATCH, DEC_SEQ, D_MODEL), 1.0),
        'norm1': 1.0 + nrm(ks[2], (DEPTH, D_MODEL), 0.02),
        'w_in': nrm(ks[3], (DEPTH, D_MODEL, IN_WIDTH), D_MODEL ** -0.5),
        'qk_gain': 1.0 + nrm(ks[4], (DEPTH, 4, HEAD_DIM), 0.02),
        'sink': nrm(ks[5], (DEPTH, B_Q_HEADS), 0.5),
        'conv_w': nrm(ks[6], (DEPTH, C_CONV, C_QKV), C_CONV ** -0.5),
        'a_log': jnp.log(jax.random.uniform(ks[7], (DEPTH, 2, C_HEADS), jnp.float32, 1.0, 16.0)),
        'dt_bias': dt + jnp.log(-jnp.expm1(-dt)),
        'o_gain': 1.0 + nrm(ks[8], (DEPTH, C_DV), 0.02),
        'w_gate': nrm(ks[10], (DEPTH, D_MODEL, N_BRANCH * D_MODEL), D_MODEL ** -0.5),
        'w_br_a': nrm(ks[11], (DEPTH, A_WIDTH, D_MODEL), A_WIDTH ** -0.5),
        'w_br_b': nrm(ks[12], (DEPTH, B_WIDTH, D_MODEL), B_WIDTH ** -0.5),
        'w_br_c': nrm(ks[13], (DEPTH, C_WIDTH, D_MODEL), C_WIDTH ** -0.5),
        'w_out': nrm(ks[14], (DEPTH, D_MODEL, D_MODEL), D_MODEL ** -0.5),
        'norm2': 1.0 + nrm(ks[15], (DEPTH, D_MODEL), 0.02),
        'w_ffn_in': nrm(ks[16], (DEPTH, D_MODEL, 2 * FFN_HIDDEN), D_MODEL ** -0.5),
        'w_ffn_out': nrm(ks[17], (DEPTH, FFN_HIDDEN, D_MODEL), FFN_HIDDEN ** -0.5),
    }


def reference(x_prompt, x_sample, norm1, w_in, qk_gain, sink, conv_w, a_log, dt_bias, o_gain, w_gate,
              w_br_a, w_br_b, w_br_c, w_out, norm2, w_ffn_in, w_ffn_out):
    y_prompt, y_sample = x_prompt, x_sample
    for l in range(DEPTH):
        layer_w = (norm1[l], w_in[l], qk_gain[l], sink[l], conv_w[l], a_log[l], dt_bias[l], o_gain[l],
                   w_gate[l], w_br_a[l], w_br_b[l], w_br_c[l], w_out[l], norm2[l], w_ffn_in[l], w_ffn_out[l])
        y_prompt = encoder_layer(y_prompt, *layer_w)
        y_sample = encoder_layer(y_sample, *layer_w)
    return (y_prompt, y_sample)
```

```python
import functools

import numpy as np
import jax
import jax.numpy as jnp
from jax import lax
from jax.experimental import pallas as pl
from jax.experimental.pallas import tpu as pltpu

F32 = jnp.float32
BF16 = jnp.bfloat16

D_MODEL = 1024
HEAD_DIM = 64
ROPE_DIM = HEAD_DIM // 4
ROPE_THETA = 500000.0
EPS = 1e-6
NEG_INF = -1e30

A_PATTERNS = ((128, 1), (512, 4), (2048, 16))
A_GROUPS = 3
A_HEADS = 8
A_WIDTH = A_HEADS * HEAD_DIM
A_QKV = A_GROUPS * A_WIDTH
B_Q_HEADS = 8
B_KV_HEADS = 2
B_HALF_WINDOW = 128
C_HEADS = 8
C_CHUNK = 64
C_CONV = 5
C_WIDTH = 512
FFN_HIDDEN = 2816
N_BRANCH = 3

LANES = 128
MXU = 256
VMEM_LIMIT = 56 * 1024 * 1024

K1_A = 3 * A_QKV
K1_B = 512 + 256 + 256
K1_C = 3 * C_WIDTH
K1_Z = C_WIDTH
K1_BA = LANES
K1_OFF_B = K1_A
K1_OFF_C = K1_OFF_B + K1_B
K1_OFF_Z = K1_OFF_C + K1_C
K1_OFF_BA = K1_OFF_Z + K1_Z
K1_WIDTH = K1_OFF_BA + K1_BA


def _dot(a, b):
    return jnp.dot(a, b, preferred_element_type=F32)


def _dot_nt(a, b):
    return lax.dot_general(a, b, (((1,), (1,)), ((), ())), preferred_element_type=F32)


def _split3(a):
    a1 = a.astype(BF16)
    r1 = a - a1.astype(F32)
    a2 = r1.astype(BF16)
    a3 = (r1 - a2.astype(F32)).astype(BF16)
    return a1, a2, a3


def _dot_data_sel(a, sel):
    a1, a2, a3 = _split3(a)
    return _dot(a1, sel) + _dot(a2, sel) + _dot(a3, sel)


def _dot_sel_data(sel, b):
    b1, b2, b3 = _split3(b)
    return _dot(sel, b1) + _dot(sel, b2) + _dot(sel, b3)


def _dot_nt_sel_data(sel, b):
    b1, b2, b3 = _split3(b)
    return _dot_nt(sel, b1) + _dot_nt(sel, b2) + _dot_nt(sel, b3)


def _rms_rows(x, gain):
    ms = jnp.mean(x * x, axis=-1, keepdims=True)
    return x * lax.rsqrt(ms + EPS) * gain


def _sigmoid(x):
    return 1.0 / (1.0 + jnp.exp(-x))


def _head_ones(n):
    r = lax.broadcasted_iota(jnp.int32, (n, n), 0) // HEAD_DIM
    c = lax.broadcasted_iota(jnp.int32, (n, n), 1) // HEAD_DIM
    return jnp.where(r == c, 1.0, 0.0).astype(BF16)


def _const_spec(shape):
    nd = len(shape)
    return pl.BlockSpec(shape, lambda *_: (0,) * nd, pipeline_mode=pl.Buffered(1))


def _inproj_kernel(x_ref, n1_ref, w_ref, cos_ref, sa_ref, sb_ref, gain_ref,
                   oa_ref, ob_ref, oc_ref, oz_ref, oba_ref):
    h = _rms_rows(x_ref[...], n1_ref[...]).astype(BF16)
    ones = _head_ones(MXU)
    cos, sa, sb = cos_ref[...], sa_ref[...], sb_ref[...]

    def qk_segment(c0, width, gain_row, out_ref, o0):
        gain = gain_ref[gain_row:gain_row + 1, :]
        for c in range(0, width, MXU):
            acc = _dot(h, w_ref[:, c0 + c:c0 + c + MXU])
            ss = _dot((acc * acc).astype(BF16), ones)
            r = lax.rsqrt(ss * (1.0 / HEAD_DIM) + EPS)
            for s in range(0, MXU, LANES):
                y = acc[:, s:s + LANES] * r[:, s:s + LANES] * gain
                rot = y * cos + pltpu.roll(y, LANES - 8, 1) * sa + pltpu.roll(y, 8, 1) * sb
                out_ref[:, o0 + c + s:o0 + c + s + LANES] = rot.astype(out_ref.dtype)

    def plain_segment(c0, width, out_ref, o0, step=512):
        for c in range(0, width, step):
            w = min(step, width - c)
            out_ref[:, o0 + c:o0 + c + w] = _dot(h, w_ref[:, c0 + c:c0 + c + w]).astype(out_ref.dtype)

    qk_segment(0, A_QKV, 0, oa_ref, 0)
    qk_segment(A_QKV, A_QKV, 1, oa_ref, A_QKV)
    plain_segment(2 * A_QKV, A_QKV, oa_ref, 2 * A_QKV)
    qk_segment(K1_OFF_B, 512, 2, ob_ref, 0)
    qk_segment(K1_OFF_B + 512, 256, 3, ob_ref, 512)
    plain_segment(K1_OFF_B + 768, 256, ob_ref, 768)
    plain_segment(K1_OFF_C, K1_C, oc_ref, 0)
    plain_segment(K1_OFF_Z, K1_Z, oz_ref, 0)
    plain_segment(K1_OFF_BA, K1_BA, oba_ref, 0)


def _inproj(x2, n1, w, cos_t, sa_t, sb_t, gains, seq, tm=256):
    t = x2.shape[0]
    ns = seq // tm
    row = lambda i: (i, 0)
    pos = lambda i: (i % ns, 0)
    return pl.pallas_call(
        _inproj_kernel,
        grid=(t // tm,),
        in_specs=[
            pl.BlockSpec((tm, D_MODEL), row),
            _const_spec((1, D_MODEL)),
            _const_spec((D_MODEL, K1_WIDTH)),
            pl.BlockSpec((tm, LANES), pos),
            pl.BlockSpec((tm, LANES), pos),
            pl.BlockSpec((tm, LANES), pos),
            _const_spec((4, LANES)),
        ],
        out_specs=[
            pl.BlockSpec((tm, K1_A), row),
            pl.BlockSpec((tm, K1_B), row),
            pl.BlockSpec((tm, K1_C), row),
            pl.BlockSpec((tm, K1_Z), row),
            pl.BlockSpec((tm, K1_BA), row),
        ],
        out_shape=[
            jax.ShapeDtypeStruct((t, K1_A), BF16),
            jax.ShapeDtypeStruct((t, K1_B), BF16),
            jax.ShapeDtypeStruct((t, K1_C), F32),
            jax.ShapeDtypeStruct((t, K1_Z), F32),
            jax.ShapeDtypeStruct((t, K1_BA), F32),
        ],
        compiler_params=pltpu.CompilerParams(dimension_semantics=("parallel",), vmem_limit_bytes=VMEM_LIMIT),
    )(x2, n1, w, cos_t, sa_t, sb_t, gains)


def _attn_kernel(*refs, tq, halo, sub, length, kv_shared, has_sink, has_lse):
    refs = list(refs)
    sink_ref = refs.pop(0) if has_sink else None
    q_ref, kp_ref, kc_ref, kn_ref, vp_ref, vc_ref, vn_ref, o_ref = refs[:8]
    rest = refs[8:]
    lse_ref = rest.pop(0) if has_lse else None
    kbuf, vbuf = rest
    n = pl.program_id(2)
    nk = sub + 2 * halo

    kbuf[0:halo, :] = kp_ref[0]
    kbuf[halo:halo + tq, :] = kc_ref[0]
    kbuf[halo + tq:, :] = kn_ref[0]
    vbuf[0:halo, :] = vp_ref[0]
    vbuf[halo:halo + tq, :] = vc_ref[0]
    vbuf[halo + tq:, :] = vn_ref[0]

    rows = lax.broadcasted_iota(jnp.int32, (sub, nk), 0)
    cols = lax.broadcasted_iota(jnp.int32, (sub, nk), 1)
    rel = cols - halo - rows
    band = (rel <= halo) & (rel >= -halo)
    low_lane = lax.broadcasted_iota(jnp.int32, (sub, LANES), 1) < HEAD_DIM

    for s in range(tq // sub):
        kpos = n * tq + (s * sub - halo) + cols
        mask = band & (kpos >= 0) & (kpos < length)
        for hp in range(4):
            qp = q_ref[0, s * sub:(s + 1) * sub, hp * LANES:(hp + 1) * LANES]
            kcol = (hp // 2 if kv_shared else hp) * LANES
            kk = kbuf[s * sub:s * sub + nk, kcol:kcol + LANES]
            vv = vbuf[s * sub:s * sub + nk, kcol:kcol + LANES]
            outs, lses = [], []
            for half in range(2):
                qm = jnp.where(low_lane == (half == 0), qp, jnp.zeros_like(qp))
                sc = jnp.where(mask, _dot_nt(qm, kk), NEG_INF)
                m = jnp.max(sc, axis=-1, keepdims=True)
                if has_sink:
                    snk = sink_ref[hp * 2 + half]
                    m = jnp.maximum(m, snk)
                p = jnp.exp(sc - m)
                den = jnp.sum(p, axis=-1, keepdims=True)
                if has_sink:
                    den = den + jnp.exp(snk - m)
                o = _dot(p.astype(BF16), vv) / den
                outs.append(o)
                if has_lse:
                    lses.append(jnp.broadcast_to(m + jnp.log(den), (sub, LANES)))
            o_ref[0, s * sub:(s + 1) * sub, hp * LANES:(hp + 1) * LANES] = (
                jnp.where(low_lane, outs[0], outs[1]).astype(o_ref.dtype))
            if has_lse:
                lse_ref[0, s * sub:(s + 1) * sub, hp * LANES:(hp + 1) * LANES] = (
                    jnp.where(low_lane, lses[0], lses[1]))


def _banded_attention(qkv, *, batch, length, nres, q_cb, k_cb, v_cb, kv_width, out_cols, halo, tq, sub,
                      kv_shared, sink=None, has_lse=False, out_dtype=F32):
    nb = length // tq
    per = tq // halo
    last = length // halo - 1
    q_spec = pl.BlockSpec((1, tq, 512), lambda b, r, n: (b, n, q_cb(r)))

    def kv_specs(cb):
        return [
            pl.BlockSpec((1, halo, kv_width), lambda b, r, n: (b, jnp.maximum(n * per - 1, 0), cb(r))),
            pl.BlockSpec((1, tq, kv_width), lambda b, r, n: (b, n, cb(r))),
            pl.BlockSpec((1, halo, kv_width), lambda b, r, n: (b, jnp.minimum((n + 1) * per, last), cb(r))),
        ]

    in_specs = [q_spec] + kv_specs(k_cb) + kv_specs(v_cb)
    args = [qkv] * 7
    if sink is not None:
        in_specs = [pl.BlockSpec(memory_space=pltpu.SMEM)] + in_specs
        args = [sink] + args
    o_spec = pl.BlockSpec((1, tq, 512), lambda b, r, n: (b, n, r))
    out_specs = [o_spec]
    out_shape = [jax.ShapeDtypeStruct((batch, length, out_cols), out_dtype)]
    if has_lse:
        out_specs.append(o_spec)
        out_shape.append(jax.ShapeDtypeStruct((batch, length, out_cols), F32))
    kern = functools.partial(_attn_kernel, tq=tq, halo=halo, sub=sub, length=length, kv_shared=kv_shared,
                             has_sink=sink is not None, has_lse=has_lse)
    return pl.pallas_call(
        kern,
        grid=(batch, nres, nb),
        in_specs=in_specs,
        out_specs=out_specs,
        out_shape=out_shape,
        scratch_shapes=[pltpu.VMEM((tq + 2 * halo, kv_width), BF16), pltpu.VMEM((tq + 2 * halo, kv_width), BF16)],
        compiler_params=pltpu.CompilerParams(dimension_semantics=("parallel", "parallel", "parallel"),
                                             vmem_limit_bytes=VMEM_LIMIT),
    )(*args)


def _mixer_a(qkv_a, batch, seq):
    res = []
    for g, (window, dil) in enumerate(A_PATTERNS):
        assert (window // 2) // dil == 64 and seq % (dil * 256) == 0
        length = seq // dil
        ncb = K1_A // 512
        o, lse = _banded_attention(
            qkv_a.reshape(batch, length, dil * K1_A), batch=batch, length=length, nres=dil,
            q_cb=lambda r, g=g: r * ncb + g, k_cb=lambda r, g=g: r * ncb + 3 + g, v_cb=lambda r, g=g: r * ncb + 6 + g,
            kv_width=512, out_cols=dil * 512, halo=64, tq=256, sub=128, kv_shared=False, has_lse=True)
        res.append((o.reshape(batch * seq, 512), lse.reshape(batch * seq, 512)))
    return res


def _mixer_b(qkv_b, sink, batch, seq):
    (o,) = _banded_attention(
        qkv_b.reshape(batch, seq, K1_B), batch=batch, length=seq, nres=1,
        q_cb=lambda r: 0, k_cb=lambda r: 2, v_cb=lambda r: 3, kv_width=256, out_cols=512,
        halo=B_HALF_WINDOW, tq=512, sub=128, kv_shared=True, sink=sink, out_dtype=BF16)
    return o.reshape(batch * seq, 512)


C_TILE = 256
C_HALO = 8
GROUP = 256


def _deltanet_kernel(xfp, xfc, xfn, baf, xbp, xbc, xbn, bab, convw_ref, alog_ref, dtb_ref,
                     of_ref, ob_ref, ext_scr, q_scr, k_scr, v_scr, beta_scr, grep_scr, g_scr, state_scr):
    n = pl.program_id(1)
    nt = pl.num_programs(1)
    nchunk = C_TILE // C_CHUNK

    @pl.when(n == 0)
    def _():
        state_scr[...] = jnp.zeros_like(state_scr)

    ones = _head_ones(GROUP)
    lane512 = lax.broadcasted_iota(jnp.int32, (LANES, 512), 1) // HEAD_DIM
    krow512 = lax.broadcasted_iota(jnp.int32, (LANES, 512), 0)

    def prologue(d, xp, xc, xn, ba_ref, tile):
        ext_scr[0:C_HALO, :] = jnp.where(tile > 0, xp[0], 0.0)
        ext_scr[C_HALO:C_HALO + C_TILE, :] = xc[0]
        ext_scr[C_HALO + C_TILE:, :] = jnp.where(tile < nt - 1, xn[0], 0.0)
        acc = None
        for tap in range(C_CONV):
            off = C_HALO - C_CONV // 2 + tap
            term = ext_scr[off:off + C_TILE, :] * convw_ref[tap:tap + 1, :]
            acc = term if acc is None else acc + term
        y = acc * _sigmoid(acc)
        for gi in range(2):
            sl = slice(gi * GROUP, (gi + 1) * GROUP)
            q = y[:, sl]
            k = y[:, 512 + gi * GROUP:512 + (gi + 1) * GROUP]
            ssq = _dot((q * q).astype(BF16), ones)
            ssk = _dot((k * k).astype(BF16), ones)
            q_scr[d, :, sl] = q * lax.rsqrt(ssq + EPS) * (HEAD_DIM ** -0.5)
            k_scr[d, :, sl] = k * lax.rsqrt(ssk + EPS)
        v_scr[d] = y[:, 1024:]
        ba = ba_ref[0]
        beta_all = _sigmoid(ba)
        zz = ba + dtb_ref[...]
        softplus = jnp.maximum(zz, 0.0) + jnp.log(1.0 + jnp.exp(-jnp.abs(zz)))
        g_all = -jnp.exp(alog_ref[...]) * softplus
        e_beta = jnp.where(krow512 == lane512 + d * C_HEADS, 1.0, 0.0).astype(BF16)
        e_g = jnp.where(krow512 == lane512 + (2 + d) * C_HEADS, 1.0, 0.0).astype(BF16)
        beta_scr[d] = _dot_data_sel(beta_all, e_beta)
        grep_scr[d] = _dot_data_sel(g_all, e_g)
        g_scr[d] = g_all

    prologue(0, xfp, xfc, xfn, baf, n)
    prologue(1, xbp, xbc, xbn, bab, nt - 1 - n)

    ii = lax.broadcasted_iota(jnp.int32, (C_CHUNK, 512), 0)
    jj = lax.broadcasted_iota(jnp.int32, (C_CHUNK, 512), 1) % C_CHUNK
    ti = lax.broadcasted_iota(jnp.int32, (C_CHUNK, C_CHUNK), 0)
    tj = lax.broadcasted_iota(jnp.int32, (C_CHUNK, C_CHUNK), 1)
    ig = lax.broadcasted_iota(jnp.int32, (C_CHUNK, GROUP), 0)
    jg = lax.broadcasted_iota(jnp.int32, (C_CHUNK, GROUP), 1) % C_CHUNK
    incl = (ii >= jj, ii <= jj)
    incl_g = (ig >= jg, ig <= jg)
    strict_g = (ig > jg, ig < jg)
    tri = (jnp.where(tj <= ti, 1.0, 0.0).astype(BF16), jnp.where(tj >= ti, 1.0, 0.0).astype(BF16))
    eye = jnp.where(ig == jg, 1.0, 0.0)
    diag16 = (ig // 16) == (jg // 16)
    hm_r = lax.broadcasted_iota(jnp.int32, (GROUP, GROUP), 0) // HEAD_DIM
    hm_c = lax.broadcasted_iota(jnp.int32, (GROUP, GROUP), 1) // HEAD_DIM
    headmask = hm_r == hm_c
    ones_row = jnp.ones((C_CHUNK, LANES), BF16)
    st_row = lax.broadcasted_iota(jnp.int32, (512, LANES), 0) // C_CHUNK
    st_col = lax.broadcasted_iota(jnp.int32, (512, LANES), 1)

    def blockdiag(x):
        xb = x.astype(BF16)
        return jnp.where(headmask, jnp.concatenate([xb, xb, xb, xb], axis=0), jnp.zeros((), BF16))

    def mm(p, q):
        return _dot(p.astype(BF16), blockdiag(q))

    def unit_inverse(l):
        dg = jnp.where(diag16, l, 0.0)
        og = l - dg
        d2 = mm(dg, dg)
        d4 = mm(d2, d2)
        d8 = mm(d4, d4)
        di = eye - dg
        di = di + mm(di, d2)
        di = di + mm(di, d4)
        di = di + mm(di, d8)
        nn = mm(di, og)
        n2 = mm(nn, nn)
        m = eye - nn
        m = m + mm(m, n2)
        return mm(m, di)

    def chunk(d, r0, o_ref):
        rows = pl.ds(r0, C_CHUNK)
        qc, kc, vc = q_scr[d, rows, :], k_scr[d, rows, :], v_scr[d, rows, :]
        beta = beta_scr[d, rows, :]
        gi_ = _dot_sel_data(tri[d], grep_scr[d, rows, :])
        gc = _dot_sel_data(tri[d], g_scr[d, rows, :])
        stack = jnp.where(st_col == st_row + (2 + d) * C_HEADS, jnp.concatenate([gc] * C_HEADS, axis=0), 0.0)
        gj = _dot_nt_sel_data(ones_row, stack)
        decay = jnp.exp(jnp.where(incl[d], gi_ - gj, NEG_INF))
        eg = jnp.exp(gi_)
        last = C_CHUNK - 1 if d == 0 else 0
        glrow = gi_[last:last + 1, :]
        kd = kc * jnp.exp(glrow - gi_)
        qg = qc * eg
        kbg = kc * beta * eg
        vb = vc * beta
        gl = jnp.exp(glrow)
        for gidx in range(2):
            sl = slice(gidx * GROUP, (gidx + 1) * GROUP)
            kg = kc[:, sl].astype(BF16)
            kbd = blockdiag(kc[:, sl])
            kk = _dot_nt(kg, kbd)
            qk = _dot_nt(qc[:, sl].astype(BF16), kbd)
            lm = jnp.where(strict_g[d], kk * beta[:, sl] * decay[:, sl], 0.0)
            qkm = jnp.where(incl_g[d], qk * decay[:, sl], 0.0)
            tinv = unit_inverse(lm)
            u = mm(tinv, vb[:, sl])
            w = mm(tinv, kbg[:, sl])
            state = state_scr[d, gidx]
            sb = state.astype(BF16)
            v_new = u - _dot(w.astype(BF16), sb)
            o = _dot(qg[:, sl].astype(BF16), sb) + mm(qkm, v_new)
            upd = _dot(kd[:, sl].T.astype(BF16), v_new.astype(BF16))
            state_scr[d, gidx] = state * gl[:, sl] + jnp.where(headmask, upd, 0.0)
            o_ref[0, rows, sl] = o

    def body(c, carry):
        chunk(0, pl.multiple_of(c * C_CHUNK, C_CHUNK), of_ref)
        chunk(1, pl.multiple_of((nchunk - 1 - c) * C_CHUNK, C_CHUNK), ob_ref)
        return carry

    lax.fori_loop(0, nchunk, body, 0)


def _deltanet(xc, ba, conv_w, alog_row, dtb_row, batch, seq):
    nt = seq // C_TILE
    per = C_TILE // C_HALO
    last = seq // C_HALO - 1
    fwd = lambda n: n
    bwd = lambda n: nt - 1 - n

    def specs(tile):
        return [
            pl.BlockSpec((1, C_HALO, K1_C), lambda b, n: (b, jnp.maximum(tile(n) * per - 1, 0), 0)),
            pl.BlockSpec((1, C_TILE, K1_C), lambda b, n: (b, tile(n), 0)),
            pl.BlockSpec((1, C_HALO, K1_C), lambda b, n: (b, jnp.minimum((tile(n) + 1) * per, last), 0)),
            pl.BlockSpec((1, C_TILE, K1_BA), lambda b, n: (b, tile(n), 0)),
        ]

    out_sds = jax.ShapeDtypeStruct((batch, seq, C_WIDTH), F32)
    return pl.pallas_call(
        _deltanet_kernel,
        grid=(batch, nt),
        in_specs=specs(fwd) + specs(bwd) + [_const_spec((C_CONV, K1_C)), _const_spec((1, LANES)),
                                            _const_spec((1, LANES))],
        out_specs=[pl.BlockSpec((1, C_TILE, C_WIDTH), lambda b, n: (b, n, 0)),
                   pl.BlockSpec((1, C_TILE, C_WIDTH), lambda b, n: (b, nt - 1 - n, 0))],
        out_shape=[out_sds, out_sds],
        scratch_shapes=[
            pltpu.VMEM((C_TILE + 2 * C_HALO, K1_C), F32),
            pltpu.VMEM((2, C_TILE, C_WIDTH), F32),
            pltpu.VMEM((2, C_TILE, C_WIDTH), F32),
            pltpu.VMEM((2, C_TILE, C_WIDTH), F32),
            pltpu.VMEM((2, C_TILE, C_WIDTH), F32),
            pltpu.VMEM((2, C_TILE, C_WIDTH), F32),
            pltpu.VMEM((2, C_TILE, LANES), F32),
            pltpu.VMEM((2, 2, GROUP, GROUP), F32),
        ],
        compiler_params=pltpu.CompilerParams(dimension_semantics=("parallel", "arbitrary"),
                                             vmem_limit_bytes=VMEM_LIMIT),
    )(xc, xc, xc, ba, xc, xc, xc, ba, conv_w, alog_row, dtb_row)


def _merge_kernel(x_ref, n1_ref, oa0, oa1, oa2, la0, la1, la2, ob_ref, ocf_ref, ocb_ref, z_ref, ogain_ref,
                  wg_ref, wba_ref, wbb_ref, wbc_ref, wo_ref, out_ref):
    x = x_ref[...]
    h = _rms_rows(x, n1_ref[...]).astype(BF16)
    l0, l1, l2 = la0[...], la1[...], la2[...]
    m = jnp.maximum(jnp.maximum(l0, l1), l2)
    e0, e1, e2 = jnp.exp(l0 - m), jnp.exp(l1 - m), jnp.exp(l2 - m)
    o_a = (e0 * oa0[...] + e1 * oa1[...] + e2 * oa2[...]) / (e0 + e1 + e2)
    oc = ocf_ref[...] + ocb_ref[...]
    ones = _head_ones(MXU)
    z = z_ref[...]
    parts = []
    for c in range(0, C_WIDTH, MXU):
        blk = oc[:, c:c + MXU]
        ss = _dot((blk * blk).astype(BF16), ones)
        zz = z[:, c:c + MXU]
        parts.append(blk * lax.rsqrt(ss * (1.0 / HEAD_DIM) + EPS) * ogain_ref[:, c:c + MXU] * (zz * _sigmoid(zz)))
    o_c = jnp.concatenate(parts, axis=1)
    merged = None
    for br, (o_br, w_ref) in enumerate(((o_a, wba_ref), (ob_ref[...], wbb_ref), (o_c, wbc_ref))):
        gate = _sigmoid(_dot(h, wg_ref[:, br * D_MODEL:(br + 1) * D_MODEL]))
        term = gate * _dot(o_br.astype(BF16), w_ref[...])
        merged = term if merged is None else merged + term
    out_ref[...] = x + _dot(merged.astype(BF16), wo_ref[...])


def _merge(x2, n1, a_res, o_b, o_cf, o_cb, z, ogain, wg, wba, wbb, wbc, wo, tm=256):
    t = x2.shape[0]
    row = lambda i: (i, 0)
    wide = pl.BlockSpec((tm, D_MODEL), row)
    half = pl.BlockSpec((tm, 512), row)
    (o0, l0), (o1, l1), (o2, l2) = a_res
    return pl.pallas_call(
        _merge_kernel,
        grid=(t // tm,),
        in_specs=[wide, _const_spec((1, D_MODEL))] + [half] * 10 + [
            _const_spec((1, C_WIDTH)), _const_spec((D_MODEL, N_BRANCH * D_MODEL)), _const_spec((512, D_MODEL)),
            _const_spec((512, D_MODEL)), _const_spec((512, D_MODEL)), _const_spec((D_MODEL, D_MODEL))],
        out_specs=wide,
        out_shape=jax.ShapeDtypeStruct((t, D_MODEL), F32),
        compiler_params=pltpu.CompilerParams(dimension_semantics=("parallel",), vmem_limit_bytes=VMEM_LIMIT),
    )(x2, n1, o0, o1, o2, l0, l1, l2, o_b, o_cf, o_cb, z, ogain, wg, wba, wbb, wbc, wo)


def _ffn_kernel(x_ref, n2_ref, wi_ref, wo_ref, out_ref):
    x = x_ref[...]
    h = _rms_rows(x, n2_ref[...]).astype(BF16)
    acc = x
    for c in range(0, FFN_HIDDEN, MXU):
        gate = _dot(h, wi_ref[:, c:c + MXU])
        up = _dot(h, wi_ref[:, FFN_HIDDEN + c:FFN_HIDDEN + c + MXU])
        act = (gate * _sigmoid(gate) * up).astype(BF16)
        acc = acc + _dot(act, wo_ref[c:c + MXU, :])
    out_ref[...] = acc


def _ffn(x2, n2, wi, wo, tm=256):
    t = x2.shape[0]
    row = lambda i: (i, 0)
    return pl.pallas_call(
        _ffn_kernel,
        grid=(t // tm,),
        in_specs=[pl.BlockSpec((tm, D_MODEL), row), _const_spec((1, D_MODEL)),
                  _const_spec((D_MODEL, 2 * FFN_HIDDEN)), _const_spec((FFN_HIDDEN, D_MODEL))],
        out_specs=pl.BlockSpec((tm, D_MODEL), row),
        out_shape=jax.ShapeDtypeStruct((t, D_MODEL), F32),
        compiler_params=pltpu.CompilerParams(dimension_semantics=("parallel",), vmem_limit_bytes=VMEM_LIMIT),
    )(x2, n2, wi, wo)


def _rope_tables(seq):
    pos = jnp.arange(seq, dtype=F32)
    inv_freq = jnp.power(jnp.float32(ROPE_THETA), -jnp.arange(0, ROPE_DIM, 2, dtype=F32) / ROPE_DIM)
    ang = pos[:, None] * inv_freq[None, :]
    cos, sin = jnp.cos(ang), jnp.sin(ang)
    half = ROPE_DIM // 2
    one = jnp.ones((seq, HEAD_DIM - ROPE_DIM), F32)
    zero = jnp.zeros((seq, HEAD_DIM - half), F32)
    cos_h = jnp.concatenate([cos, cos, one], axis=1)
    sa_h = jnp.concatenate([-sin, zero], axis=1)
    sb_h = jnp.concatenate([jnp.zeros((seq, half), F32), sin, zero[:, half:]], axis=1)
    tile = lambda t: jnp.concatenate([t, t], axis=1)
    return tile(cos_h), tile(sa_h), tile(sb_h)


def _pack_layer(l, norm1, w_in, qk_gain, sink, conv_w, a_log, dt_bias, o_gain, w_gate, w_br_a, w_br_b, w_br_c,
                w_out, norm2, w_ffn_in, w_ffn_out):
    wi = w_in[l]
    kb0, kb1 = wi[:, 5120:5184], wi[:, 5184:5248]
    vb0, vb1 = wi[:, 5248:5312], wi[:, 5312:5376]
    w1 = jnp.concatenate([
        wi[:, :5120], kb0, kb0, kb1, kb1, vb0, vb0, vb1, vb1, wi[:, 5376:7424],
        wi[:, 7424:7456], jnp.zeros((D_MODEL, K1_BA - 32), F32)], axis=1).astype(BF16)
    scale = HEAD_DIM ** -0.5
    gains = jnp.stack([qk_gain[l, 0] * scale, qk_gain[l, 1], qk_gain[l, 2] * scale, qk_gain[l, 3]])
    gains = jnp.concatenate([gains, gains], axis=1)
    pad = lambda v: jnp.concatenate([jnp.zeros((16,), F32), v.reshape(-1), jnp.zeros((LANES - 32,), F32)])[None, :]
    return dict(
        n1=norm1[l][None, :], w1=w1, gains=gains, sink=sink[l], conv_w=conv_w[l],
        alog=pad(a_log[l]), dtb=pad(dt_bias[l]), ogain=jnp.tile(o_gain[l], C_HEADS)[None, :],
        wg=w_gate[l].astype(BF16), wba=w_br_a[l].astype(BF16), wbb=w_br_b[l].astype(BF16),
        wbc=w_br_c[l].astype(BF16), wo=w_out[l].astype(BF16), n2=norm2[l][None, :],
        wfi=w_ffn_in[l].astype(BF16), wfo=w_ffn_out[l].astype(BF16))


def _layer(x2, p, tabs, batch, seq):
    qkv_a, qkv_b, xc, z, ba = _inproj(x2, p["n1"], p["w1"], *tabs, p["gains"], seq)
    a_res = _mixer_a(qkv_a, batch, seq)
    o_b = _mixer_b(qkv_b, p["sink"], batch, seq)
    o_cf, o_cb = _deltanet(xc.reshape(batch, seq, K1_C), ba.reshape(batch, seq, K1_BA), p["conv_w"], p["alog"],
                           p["dtb"], batch, seq)
    t = batch * seq
    x2 = _merge(x2, p["n1"], a_res, o_b, o_cf.reshape(t, C_WIDTH), o_cb.reshape(t, C_WIDTH), z, p["ogain"],
                p["wg"], p["wba"], p["wbb"], p["wbc"], p["wo"])
    return _ffn(x2, p["n2"], p["wfi"], p["wfo"])


def kernel(x_prompt, x_sample, norm1, w_in, qk_gain, sink, conv_w, a_log, dt_bias, o_gain, w_gate, w_br_a, w_br_b,
           w_br_c, w_out, norm2, w_ffn_in, w_ffn_out):
    depth = w_in.shape[0]
    layers = [_pack_layer(l, norm1, w_in, qk_gain, sink, conv_w, a_log, dt_bias, o_gain, w_gate, w_br_a, w_br_b,
                          w_br_c, w_out, norm2, w_ffn_in, w_ffn_out) for l in range(depth)]
    outs = []
    for x in (x_prompt, x_sample):
        batch, seq, _ = x.shape
        tabs = _rope_tables(seq)
        x2 = x.reshape(batch * seq, D_MODEL)
        for p in layers:
            x2 = _layer(x2, p, tabs, batch, seq)
        outs.append(x2.reshape(batch, seq, D_MODEL))
    return tuple(outs)
```

```python
import functools

import jax
import jax.numpy as jnp
from jax import lax
from jax.experimental import pallas as pl
from jax.experimental.pallas import tpu as pltpu

F32 = jnp.float32
BF16 = jnp.bfloat16

D_MODEL = 1024
HEAD_DIM = 64
ROPE_DIM = HEAD_DIM // 4
ROPE_THETA = 500000.0
EPS = 1e-6
NEG_INF = -1e30

A_PATTERNS = ((128, 1), (512, 4), (2048, 16))
A_DILATIONS = tuple(d for _, d in A_PATTERNS)
A_GROUPS = 3
A_WIDTH = 512
A_QKV = A_GROUPS * A_WIDTH
A_HALO = 64
B_HALF_WINDOW = 128
C_HEADS = 8
C_CHUNK = 64
C_CONV = 5
C_WIDTH = 512
FFN_HIDDEN = 2816
N_BRANCH = 3

LANES = 128
MXU = 256
VMEM_LIMIT = 56 * 1024 * 1024
TOKEN_TILE = 256

K1_A = 3 * A_QKV
K1_B = 512 + 256 + 256
K1_C = 3 * C_WIDTH
K1_Z = C_WIDTH
K1_BA = LANES
K1_OFF_B = K1_A
K1_OFF_C = K1_OFF_B + K1_B
K1_OFF_Z = K1_OFF_C + K1_C
K1_OFF_BA = K1_OFF_Z + K1_Z
K1_WIDTH = K1_OFF_BA + K1_BA


def _dot(a, b):
    return jnp.dot(a, b, preferred_element_type=F32)


def _dot_nt(a, b):
    return lax.dot_general(a, b, (((1,), (1,)), ((), ())), preferred_element_type=F32)


def _dot_tn(a, b):
    return lax.dot_general(a, b, (((0,), (0,)), ((), ())), preferred_element_type=F32)


def _split3(a):
    a1 = a.astype(BF16)
    r1 = a - a1.astype(F32)
    a2 = r1.astype(BF16)
    a3 = (r1 - a2.astype(F32)).astype(BF16)
    return a1, a2, a3


def _dot_data_sel(a, sel):
    a1, a2, a3 = _split3(a)
    return _dot(a1, sel) + _dot(a2, sel) + _dot(a3, sel)


def _dot_sel_data(sel, b):
    b1, b2, b3 = _split3(b)
    return _dot(sel, b1) + _dot(sel, b2) + _dot(sel, b3)


def _rms_rows(x, gain):
    ms = jnp.mean(x * x, axis=-1, keepdims=True)
    return x * lax.rsqrt(ms + EPS) * gain


def _sigmoid(x):
    return 1.0 / (1.0 + jnp.exp(-x))


def _head_ones(n):
    r = lax.broadcasted_iota(jnp.int32, (n, n), 0) // HEAD_DIM
    c = lax.broadcasted_iota(jnp.int32, (n, n), 1) // HEAD_DIM
    return jnp.where(r == c, 1.0, 0.0).astype(BF16)


def _const_spec(shape):
    nd = len(shape)
    return pl.BlockSpec(shape, lambda *_: (0,) * nd, pipeline_mode=pl.Buffered(1))


def _params(*sem):
    return pltpu.CompilerParams(dimension_semantics=sem, vmem_limit_bytes=VMEM_LIMIT)


def _inproj_kernel(x_ref, n1_ref, w_ref, cos_ref, sa_ref, sb_ref, gain_ref,
                   a0_ref, a1_ref, a2_ref, ob_ref, oc_ref, oz_ref, oba_ref, deint_scr):
    tm = x_ref.shape[0]
    h = _rms_rows(x_ref[...], n1_ref[...]).astype(BF16)
    ones = _head_ones(MXU)
    cos, sa, sb = cos_ref[...], sa_ref[...], sb_ref[...]
    a_refs = (a0_ref, a1_ref, a2_ref)
    slot = [0]

    def qk_norm_rope(acc, gain):
        ss = _dot((acc * acc).astype(BF16), ones)
        r = lax.rsqrt(ss * (1.0 / HEAD_DIM) + EPS)
        outs = []
        for s in range(0, MXU, LANES):
            y = acc[:, s:s + LANES] * r[:, s:s + LANES] * gain
            outs.append(y * cos + pltpu.roll(y, LANES - 8, 1) * sa + pltpu.roll(y, 8, 1) * sb)
        return outs

    def emit_a(g, col, val):
        d = A_DILATIONS[g]
        if d == 1:
            a0_ref[:, col:col + LANES] = val.astype(BF16)
            return
        s = slot[0]
        slot[0] = (s + 1) % deint_scr.shape[0]
        deint_scr[s] = val
        for r in range(d):
            a_refs[g][0, r, :, col:col + LANES] = deint_scr[s, pl.ds(r, tm // d, stride=d), :].astype(BF16)

    for kind in range(3):
        for g in range(A_GROUPS):
            for c in range(0, A_WIDTH, MXU):
                acc = _dot(h, w_ref[:, kind * A_QKV + g * A_WIDTH + c:kind * A_QKV + g * A_WIDTH + c + MXU])
                if kind < 2:
                    parts = qk_norm_rope(acc, gain_ref[kind:kind + 1, :])
                else:
                    parts = [acc[:, :LANES], acc[:, LANES:]]
                for i, part in enumerate(parts):
                    emit_a(g, kind * A_WIDTH + c + i * LANES, part)

    def qk_segment(c0, width, gain_row, out_ref, o0):
        for c in range(0, width, MXU):
            parts = qk_norm_rope(_dot(h, w_ref[:, c0 + c:c0 + c + MXU]), gain_ref[gain_row:gain_row + 1, :])
            for i, part in enumerate(parts):
                out_ref[:, o0 + c + i * LANES:o0 + c + (i + 1) * LANES] = part.astype(out_ref.dtype)

    def plain_segment(c0, width, out_ref, o0, step=512):
        for c in range(0, width, step):
            w = min(step, width - c)
            out_ref[:, o0 + c:o0 + c + w] = _dot(h, w_ref[:, c0 + c:c0 + c + w]).astype(out_ref.dtype)

    qk_segment(K1_OFF_B, 512, 2, ob_ref, 0)
    qk_segment(K1_OFF_B + 512, 256, 3, ob_ref, 512)
    plain_segment(K1_OFF_B + 768, 256, ob_ref, 768)
    plain_segment(K1_OFF_C, K1_C, oc_ref, 0)
    plain_segment(K1_OFF_Z, K1_Z, oz_ref, 0)
    plain_segment(K1_OFF_BA, K1_BA, oba_ref, 0)


def _inproj(x2, n1, w, cos_t, sa_t, sb_t, gains, batch, seq):
    tm = TOKEN_TILE
    t = x2.shape[0]
    ns = seq // tm
    row = lambda i: (i, 0)
    pos = lambda i: (i % ns, 0)
    d1, d2 = A_DILATIONS[1], A_DILATIONS[2]
    res = lambda i: (i // ns, 0, i % ns, 0)
    return pl.pallas_call(
        _inproj_kernel,
        name="inproj",
        grid=(t // tm,),
        in_specs=[
            pl.BlockSpec((tm, D_MODEL), row),
            _const_spec((1, D_MODEL)),
            _const_spec((D_MODEL, K1_WIDTH)),
            pl.BlockSpec((tm, LANES), pos),
            pl.BlockSpec((tm, LANES), pos),
            pl.BlockSpec((tm, LANES), pos),
            _const_spec((4, LANES)),
        ],
        out_specs=[
            pl.BlockSpec((tm, A_QKV), row),
            pl.BlockSpec((1, d1, tm // d1, A_QKV), res),
            pl.BlockSpec((1, d2, tm // d2, A_QKV), res),
            pl.BlockSpec((tm, K1_B), row),
            pl.BlockSpec((tm, K1_C), row),
            pl.BlockSpec((tm, K1_Z), row),
            pl.BlockSpec((tm, K1_BA), row),
        ],
        out_shape=[
            jax.ShapeDtypeStruct((t, A_QKV), BF16),
            jax.ShapeDtypeStruct((batch, d1, seq // d1, A_QKV), BF16),
            jax.ShapeDtypeStruct((batch, d2, seq // d2, A_QKV), BF16),
            jax.ShapeDtypeStruct((t, K1_B), BF16),
            jax.ShapeDtypeStruct((t, K1_C), F32),
            jax.ShapeDtypeStruct((t, K1_Z), F32),
            jax.ShapeDtypeStruct((t, K1_BA), F32),
        ],
        scratch_shapes=[pltpu.VMEM((4, tm, LANES), F32)],
        compiler_params=_params("parallel"),
    )(x2, n1, w, cos_t, sa_t, sb_t, gains)


def _attn_kernel(*refs, tq, halo, sub, length, kv_shared, has_sink, has_lse):
    refs = list(refs)
    sink_ref = refs.pop(0) if has_sink else None
    q_ref, kp_ref, kc_ref, kn_ref, vp_ref, vc_ref, vn_ref, o_ref = refs[:8]
    rest = refs[8:]
    lse_ref = rest.pop(0) if has_lse else None
    kbuf, vbuf = rest
    n = pl.program_id(2)
    nk = sub + 2 * halo

    kbuf[0:halo, :] = kp_ref[0, 0]
    kbuf[halo:halo + tq, :] = kc_ref[0, 0]
    kbuf[halo + tq:, :] = kn_ref[0, 0]
    vbuf[0:halo, :] = vp_ref[0, 0]
    vbuf[halo:halo + tq, :] = vc_ref[0, 0]
    vbuf[halo + tq:, :] = vn_ref[0, 0]

    rows = lax.broadcasted_iota(jnp.int32, (sub, nk), 0)
    cols = lax.broadcasted_iota(jnp.int32, (sub, nk), 1)
    rel = cols - halo - rows
    band = (rel <= halo) & (rel >= -halo)
    low_lane = lax.broadcasted_iota(jnp.int32, (sub, LANES), 1) < HEAD_DIM

    for s in range(tq // sub):
        kpos = n * tq + (s * sub - halo) + cols
        mask = band & (kpos >= 0) & (kpos < length)
        for hp in range(4):
            qp = q_ref[0, 0, s * sub:(s + 1) * sub, hp * LANES:(hp + 1) * LANES]
            kcol = (hp // 2 if kv_shared else hp) * LANES
            kk = kbuf[s * sub:s * sub + nk, kcol:kcol + LANES]
            vv = vbuf[s * sub:s * sub + nk, kcol:kcol + LANES]
            outs, lses = [], []
            for half in range(2):
                qm = jnp.where(low_lane == (half == 0), qp, jnp.zeros_like(qp))
                sc = jnp.where(mask, _dot_nt(qm, kk), NEG_INF)
                m = jnp.max(sc, axis=-1, keepdims=True)
                if has_sink:
                    snk = sink_ref[hp * 2 + half]
                    m = jnp.maximum(m, snk)
                p = jnp.exp(sc - m)
                den = jnp.sum(p, axis=-1, keepdims=True)
                if has_sink:
                    den = den + jnp.exp(snk - m)
                o = _dot(p.astype(BF16), vv) / den
                outs.append(o)
                if has_lse:
                    lses.append(jnp.broadcast_to(m + jnp.log(den), (sub, LANES)))
            o_ref[0, 0, s * sub:(s + 1) * sub, hp * LANES:(hp + 1) * LANES] = (
                jnp.where(low_lane, outs[0], outs[1]).astype(o_ref.dtype))
            if has_lse:
                lse_ref[0, 0, s * sub:(s + 1) * sub, hp * LANES:(hp + 1) * LANES] = (
                    jnp.where(low_lane, lses[0], lses[1]))


def _banded_attention(qkv, *, name, q_cb, k_cb, v_cb, kv_width, halo, tq, sub, kv_shared, sink=None,
                      has_lse=False, out_dtype=F32):
    batch, nres, length, _ = qkv.shape
    nb = length // tq
    per = tq // halo
    last = length // halo - 1
    q_spec = pl.BlockSpec((1, 1, tq, 512), lambda b, r, n: (b, r, n, q_cb))

    def kv_specs(cb):
        return [
            pl.BlockSpec((1, 1, halo, kv_width), lambda b, r, n: (b, r, jnp.maximum(n * per - 1, 0), cb)),
            pl.BlockSpec((1, 1, tq, kv_width), lambda b, r, n: (b, r, n, cb)),
            pl.BlockSpec((1, 1, halo, kv_width), lambda b, r, n: (b, r, jnp.minimum((n + 1) * per, last), cb)),
        ]

    in_specs = [q_spec] + kv_specs(k_cb) + kv_specs(v_cb)
    args = [qkv] * 7
    if sink is not None:
        in_specs = [pl.BlockSpec(memory_space=pltpu.SMEM)] + in_specs
        args = [sink] + args
    o_spec = pl.BlockSpec((1, 1, tq, 512), lambda b, r, n: (b, r, n, 0))
    out_specs = [o_spec]
    out_shape = [jax.ShapeDtypeStruct((batch, nres, length, 512), out_dtype)]
    if has_lse:
        out_specs.append(o_spec)
        out_shape.append(jax.ShapeDtypeStruct((batch, nres, length, 512), F32))
    kern = functools.partial(_attn_kernel, tq=tq, halo=halo, sub=sub, length=length, kv_shared=kv_shared,
                             has_sink=sink is not None, has_lse=has_lse)
    return pl.pallas_call(
        kern,
        name=name,
        grid=(batch, nres, nb),
        in_specs=in_specs,
        out_specs=out_specs,
        out_shape=out_shape,
        scratch_shapes=[pltpu.VMEM((tq + 2 * halo, kv_width), BF16), pltpu.VMEM((tq + 2 * halo, kv_width), BF16)],
        compiler_params=_params("parallel", "parallel", "parallel"),
    )(*args)


def _mixer_a(groups):
    res = []
    for g, qkv in enumerate(groups):
        res.append(_banded_attention(qkv, name=f"attn_a{g}", q_cb=0, k_cb=1, v_cb=2, kv_width=512, halo=A_HALO,
                                     tq=256, sub=128, kv_shared=False, has_lse=True))
    return res


def _mixer_b(qkv_b, sink, batch, seq):
    (o,) = _banded_attention(
        qkv_b.reshape(batch, 1, seq, K1_B), name="attn_b", q_cb=0, k_cb=2, v_cb=3, kv_width=256,
        halo=B_HALF_WINDOW, tq=512, sub=128, kv_shared=True, sink=sink, out_dtype=BF16)
    return o.reshape(batch * seq, 512)


C_TILE = 256
C_HALO = 8
GROUP = 256
N_CHUNK = C_TILE // C_CHUNK


def _head_mask():
    r = lax.broadcasted_iota(jnp.int32, (GROUP, GROUP), 0) // HEAD_DIM
    c = lax.broadcasted_iota(jnp.int32, (GROUP, GROUP), 1) // HEAD_DIM
    return r == c


def _blockdiag(x, headmask):
    xb = x.astype(BF16)
    return jnp.where(headmask, jnp.concatenate([xb, xb, xb, xb], axis=0), jnp.zeros((), BF16))


def _delta_chunks_kernel(xp_ref, xc_ref, xn_ref, ba_ref, convw_ref, alog_ref, dtb_ref,
                         uf_ref, wf_ref, qkf_ref, qgf_ref, kdf_ref, glf_ref,
                         ub_ref, wb_ref, qkb_ref, qgb_ref, kdb_ref, glb_ref,
                         ext_scr, q_scr, k_scr, v_scr, beta_scr, gi_scr):
    n = pl.program_id(1)
    nt = pl.num_programs(1)
    outs = ((uf_ref, wf_ref, qkf_ref, qgf_ref, kdf_ref, glf_ref), (ub_ref, wb_ref, qkb_ref, qgb_ref, kdb_ref, glb_ref))

    ext_scr[0:C_HALO, :] = jnp.where(n > 0, xp_ref[0], 0.0)
    ext_scr[C_HALO:C_HALO + C_TILE, :] = xc_ref[0]
    ext_scr[C_HALO + C_TILE:, :] = jnp.where(n < nt - 1, xn_ref[0], 0.0)
    acc = None
    for tap in range(C_CONV):
        off = C_HALO - C_CONV // 2 + tap
        term = ext_scr[off:off + C_TILE, :] * convw_ref[tap:tap + 1, :]
        acc = term if acc is None else acc + term
    y = acc * _sigmoid(acc)
    ones = _head_ones(GROUP)
    for gi in range(2):
        sl = slice(gi * GROUP, (gi + 1) * GROUP)
        q = y[:, sl]
        k = y[:, 512 + gi * GROUP:512 + (gi + 1) * GROUP]
        ssq = _dot((q * q).astype(BF16), ones)
        ssk = _dot((k * k).astype(BF16), ones)
        q_scr[:, sl] = q * lax.rsqrt(ssq + EPS) * (HEAD_DIM ** -0.5)
        k_scr[:, sl] = k * lax.rsqrt(ssk + EPS)
    v_scr[...] = y[:, 1024:]

    ba = ba_ref[0]
    beta_all = _sigmoid(ba)
    zz = ba + dtb_ref[...]
    g_all = -jnp.exp(alog_ref[...]) * (jnp.maximum(zz, 0.0) + jnp.log(1.0 + jnp.exp(-jnp.abs(zz))))
    lane_head = lax.broadcasted_iota(jnp.int32, (LANES, 512), 1) // HEAD_DIM
    krow = lax.broadcasted_iota(jnp.int32, (LANES, 512), 0)
    ti = lax.broadcasted_iota(jnp.int32, (C_TILE, C_TILE), 0)
    tj = lax.broadcasted_iota(jnp.int32, (C_TILE, C_TILE), 1)
    same_chunk = (ti // C_CHUNK) == (tj // C_CHUNK)
    for d in range(2):
        e_beta = jnp.where(krow == lane_head + d * C_HEADS, 1.0, 0.0).astype(BF16)
        e_g = jnp.where(krow == lane_head + (2 + d) * C_HEADS, 1.0, 0.0).astype(BF16)
        beta_scr[d] = _dot_data_sel(beta_all, e_beta)
        g_rep = _dot_data_sel(g_all, e_g)
        tri = jnp.where(same_chunk & ((tj <= ti) if d == 0 else (tj >= ti)), 1.0, 0.0).astype(BF16)
        gi_scr[d] = _dot_sel_data(tri, g_rep)

    ii = lax.broadcasted_iota(jnp.int32, (C_CHUNK, 512), 0)
    jj = lax.broadcasted_iota(jnp.int32, (C_CHUNK, 512), 1) % C_CHUNK
    ig = lax.broadcasted_iota(jnp.int32, (C_CHUNK, GROUP), 0)
    jg = lax.broadcasted_iota(jnp.int32, (C_CHUNK, GROUP), 1) % C_CHUNK
    incl = (ii >= jj, ii <= jj)
    incl_g = (ig >= jg, ig <= jg)
    strict_g = (ig > jg, ig < jg)
    eye = jnp.where(ig == jg, 1.0, 0.0)
    diag16 = (ig // 16) == (jg // 16)
    headmask = _head_mask()
    zero_rows = jnp.zeros((LANES - C_CHUNK, LANES), F32)

    def mm(ps, qs):
        return [_dot(p.astype(BF16), _blockdiag(q, headmask)) for p, q in zip(ps, qs)]

    def add(xs, ys):
        return [x + y for x, y in zip(xs, ys)]

    def unit_inverses(ls):
        dg = [jnp.where(diag16, l, 0.0) for l in ls]
        og = [l - d for l, d in zip(ls, dg)]
        d2 = mm(dg, dg)
        d4 = mm(d2, d2)
        d8 = mm(d4, d4)
        di = [eye - d for d in dg]
        di = add(di, mm(di, d2))
        di = add(di, mm(di, d4))
        di = add(di, mm(di, d8))
        nn = mm(di, og)
        n2 = mm(nn, nn)
        m = [eye - x for x in nn]
        m = add(m, mm(m, n2))
        return mm(m, di)

    def transposed_decay(gi_):
        parts = []
        for p in range(512 // LANES):
            t = jnp.concatenate([gi_[:, p * LANES:(p + 1) * LANES], zero_rows], axis=0).T
            parts.append(t[:C_CHUNK, :] + pltpu.roll(t[C_CHUNK:, :], HEAD_DIM, 1))
        return jnp.concatenate(parts, axis=1)

    def chunks(cs):
        lms, vbs, kbgs, dest = [], [], [], []
        for c in cs:
            rows = pl.ds(pl.multiple_of(c * C_CHUNK, C_CHUNK), C_CHUNK)
            qc, kc, vc = q_scr[rows, :], k_scr[rows, :], v_scr[rows, :]
            kk, qk = [], []
            for gidx in range(2):
                sl = slice(gidx * GROUP, (gidx + 1) * GROUP)
                kbd = _blockdiag(kc[:, sl], headmask)
                kk.append(_dot_nt(kc[:, sl].astype(BF16), kbd))
                qk.append(_dot_nt(qc[:, sl].astype(BF16), kbd))
            for d in range(2):
                u_ref, w_ref, qk_ref, qg_ref, kd_ref, gl_ref = outs[d]
                beta = beta_scr[d, rows, :]
                gi_ = gi_scr[d, rows, :]
                decay = jnp.exp(jnp.where(incl[d], gi_ - transposed_decay(gi_), NEG_INF))
                eg = jnp.exp(gi_)
                last = C_CHUNK - 1 if d == 0 else 0
                glrow = gi_[last:last + 1, :]
                kd_ref[0, rows, :] = (kc * jnp.exp(glrow - gi_)).astype(BF16)
                qg_ref[0, rows, :] = (qc * eg).astype(BF16)
                gl_ref[0, pl.ds(c, 1), 0, :] = jnp.exp(glrow)
                kbg = kc * beta * eg
                vb = vc * beta
                for gidx in range(2):
                    sl = slice(gidx * GROUP, (gidx + 1) * GROUP)
                    lms.append(jnp.where(strict_g[d], kk[gidx] * beta[:, sl] * decay[:, sl], 0.0))
                    qk_ref[0, rows, sl] = jnp.where(incl_g[d], qk[gidx] * decay[:, sl], 0.0).astype(BF16)
                    vbs.append(vb[:, sl])
                    kbgs.append(kbg[:, sl])
                    dest.append((u_ref, w_ref, rows, sl))
        tinvs = unit_inverses(lms)
        us = mm(tinvs, vbs)
        ws = mm(tinvs, kbgs)
        for (u_ref, w_ref, rows, sl), u, w in zip(dest, us, ws):
            u_ref[0, rows, sl] = u
            w_ref[0, rows, sl] = w.astype(BF16)

    def body(c2, carry):
        chunks((2 * c2, 2 * c2 + 1))
        return carry

    lax.fori_loop(0, N_CHUNK // 2, body, 0)


def _delta_chunks(xc, ba, conv_w, alog_row, dtb_row):
    batch, seq, _ = xc.shape
    nt = seq // C_TILE
    per = C_TILE // C_HALO
    last = seq // C_HALO - 1
    tile = lambda b, n: (b, n, 0)
    wide = lambda dt: jax.ShapeDtypeStruct((batch, seq, C_WIDTH), dt)
    gl_sds = jax.ShapeDtypeStruct((batch, seq // C_CHUNK, 1, C_WIDTH), F32)
    out_shape = [wide(F32), wide(BF16), wide(BF16), wide(BF16), wide(BF16), gl_sds] * 2
    wide_spec = pl.BlockSpec((1, C_TILE, C_WIDTH), tile)
    gl_spec = pl.BlockSpec((1, N_CHUNK, 1, C_WIDTH), lambda b, n: (b, n, 0, 0))
    return pl.pallas_call(
        _delta_chunks_kernel,
        name="delta_chunks",
        grid=(batch, nt),
        in_specs=[
            pl.BlockSpec((1, C_HALO, K1_C), lambda b, n: (b, jnp.maximum(n * per - 1, 0), 0)),
            pl.BlockSpec((1, C_TILE, K1_C), tile),
            pl.BlockSpec((1, C_HALO, K1_C), lambda b, n: (b, jnp.minimum((n + 1) * per, last), 0)),
            pl.BlockSpec((1, C_TILE, K1_BA), tile),
            _const_spec((C_CONV, K1_C)), _const_spec((1, LANES)), _const_spec((1, LANES))],
        out_specs=[wide_spec] * 5 + [gl_spec] + [wide_spec] * 5 + [gl_spec],
        out_shape=out_shape,
        scratch_shapes=[
            pltpu.VMEM((C_TILE + 2 * C_HALO, K1_C), F32),
            pltpu.VMEM((C_TILE, C_WIDTH), F32),
            pltpu.VMEM((C_TILE, C_WIDTH), F32),
            pltpu.VMEM((C_TILE, C_WIDTH), F32),
            pltpu.VMEM((2, C_TILE, C_WIDTH), F32),
            pltpu.VMEM((2, C_TILE, C_WIDTH), F32),
        ],
        compiler_params=_params("parallel", "parallel"),
    )(xc, xc, xc, ba, conv_w, alog_row, dtb_row)


def _delta_scan_kernel(uf_ref, wf_ref, qkf_ref, qgf_ref, kdf_ref, glf_ref,
                       ub_ref, wb_ref, qkb_ref, qgb_ref, kdb_ref, glb_ref, of_ref, ob_ref, state_scr):
    n = pl.program_id(1)
    bblk = uf_ref.shape[0]
    ins = ((uf_ref, wf_ref, qkf_ref, qgf_ref, kdf_ref, glf_ref, of_ref),
           (ub_ref, wb_ref, qkb_ref, qgb_ref, kdb_ref, glb_ref, ob_ref))

    @pl.when(n == 0)
    def _():
        state_scr[...] = jnp.zeros_like(state_scr)

    headmask = _head_mask()

    def body(c, carry):
        chains = []
        for d in range(2):
            cc = c if d == 0 else N_CHUNK - 1 - c
            rows = pl.ds(pl.multiple_of(cc * C_CHUNK, C_CHUNK), C_CHUNK)
            for b in range(bblk):
                for gidx in range(2):
                    chains.append((d, b, gidx, cc, rows, slice(gidx * GROUP, (gidx + 1) * GROUP)))
        states = [state_scr[b, d, gidx] for d, b, gidx, _, _, _ in chains]
        both = [_dot(jnp.concatenate([ins[d][1][b, rows, sl], ins[d][3][b, rows, sl]], axis=0), st.astype(BF16))
                for (d, b, _, _, rows, sl), st in zip(chains, states)]
        v_new = [ins[d][0][b, rows, sl] - bo[:C_CHUNK] for (d, b, _, _, rows, sl), bo in zip(chains, both)]
        intra = [_dot(ins[d][2][b, rows, sl], _blockdiag(v, headmask)) for (d, b, _, _, rows, sl), v in zip(chains, v_new)]
        upd = [_dot_tn(ins[d][4][b, rows, sl], v.astype(BF16)) for (d, b, _, _, rows, sl), v in zip(chains, v_new)]
        for (d, b, gidx, cc, rows, sl), st, bo, it, up in zip(chains, states, both, intra, upd):
            ins[d][6][b, rows, sl] = bo[C_CHUNK:] + it
            gl = ins[d][5][b, pl.ds(cc, 1), 0, :]
            state_scr[b, d, gidx] = st * gl[:, sl] + jnp.where(headmask, up, 0.0)
        return carry

    lax.fori_loop(0, N_CHUNK, body, 0)


def _delta_scan(factors):
    batch, seq, _ = factors[0].shape
    nt = seq // C_TILE
    bblk = min(batch, 4)
    fwd = lambda b, n: (b, n, 0)
    bwd = lambda b, n: (b, nt - 1 - n, 0)

    def specs(tile):
        wide = pl.BlockSpec((bblk, C_TILE, C_WIDTH), tile)
        gl = pl.BlockSpec((bblk, N_CHUNK, 1, C_WIDTH), lambda b, n: tile(b, n) + (0,))
        return [wide] * 5 + [gl]

    out_sds = jax.ShapeDtypeStruct((batch, seq, C_WIDTH), F32)
    return pl.pallas_call(
        _delta_scan_kernel,
        name="delta_scan",
        grid=(batch // bblk, nt),
        in_specs=specs(fwd) + specs(bwd),
        out_specs=[pl.BlockSpec((bblk, C_TILE, C_WIDTH), fwd), pl.BlockSpec((bblk, C_TILE, C_WIDTH), bwd)],
        out_shape=[out_sds, out_sds],
        scratch_shapes=[pltpu.VMEM((bblk, 2, 2, GROUP, GROUP), F32)],
        compiler_params=_params("parallel", "arbitrary"),
    )(*factors)


def _merge_kernel(x_ref, n1_ref, oa0, la0, oa1, la1, oa2, la2, ob_ref, ocf_ref, ocb_ref, z_ref, ogain_ref,
                  wg_ref, wba_ref, wbb_ref, wbc_ref, wo_ref, out_ref, inter_scr):
    tm = x_ref.shape[0]
    x = x_ref[...]
    h = _rms_rows(x, n1_ref[...]).astype(BF16)

    def interleaved(ref, d, base):
        for r in range(d):
            for c in range(512 // LANES):
                inter_scr[base + c, pl.ds(r, tm // d, stride=d), :] = ref[0, r, :, c * LANES:(c + 1) * LANES]
        return jnp.concatenate([inter_scr[base + c] for c in range(512 // LANES)], axis=1)

    d1, d2 = A_DILATIONS[1], A_DILATIONS[2]
    l0, o0 = la0[...], oa0[...]
    l1, o1 = interleaved(la1, d1, 0), interleaved(oa1, d1, 4)
    l2, o2 = interleaved(la2, d2, 8), interleaved(oa2, d2, 12)
    m = jnp.maximum(jnp.maximum(l0, l1), l2)
    e0, e1, e2 = jnp.exp(l0 - m), jnp.exp(l1 - m), jnp.exp(l2 - m)
    o_a = (e0 * o0 + e1 * o1 + e2 * o2) / (e0 + e1 + e2)
    oc = ocf_ref[...] + ocb_ref[...]
    ones = _head_ones(MXU)
    z = z_ref[...]
    parts = []
    for c in range(0, C_WIDTH, MXU):
        blk = oc[:, c:c + MXU]
        ss = _dot((blk * blk).astype(BF16), ones)
        zz = z[:, c:c + MXU]
        parts.append(blk * lax.rsqrt(ss * (1.0 / HEAD_DIM) + EPS) * ogain_ref[:, c:c + MXU] * (zz * _sigmoid(zz)))
    o_c = jnp.concatenate(parts, axis=1)
    merged = None
    for br, (o_br, w_ref) in enumerate(((o_a, wba_ref), (ob_ref[...], wbb_ref), (o_c, wbc_ref))):
        gate = _sigmoid(_dot(h, wg_ref[:, br * D_MODEL:(br + 1) * D_MODEL]))
        term = gate * _dot(o_br.astype(BF16), w_ref[...])
        merged = term if merged is None else merged + term
    out_ref[...] = x + _dot(merged.astype(BF16), wo_ref[...])


def _merge(x2, n1, a_res, o_b, o_cf, o_cb, z, ogain, wg, wba, wbb, wbc, wo, seq):
    tm = TOKEN_TILE
    t = x2.shape[0]
    ns = seq // tm
    row = lambda i: (i, 0)
    res = lambda i: (i // ns, 0, i % ns, 0)
    wide = pl.BlockSpec((tm, D_MODEL), row)
    half = pl.BlockSpec((tm, 512), row)
    d1, d2 = A_DILATIONS[1], A_DILATIONS[2]
    res1 = pl.BlockSpec((1, d1, tm // d1, 512), res)
    res2 = pl.BlockSpec((1, d2, tm // d2, 512), res)
    (o0, l0), (o1, l1), (o2, l2) = a_res
    return pl.pallas_call(
        _merge_kernel,
        name="merge",
        grid=(t // tm,),
        in_specs=[wide, _const_spec((1, D_MODEL)), half, half, res1, res1, res2, res2] + [half] * 4 + [
            _const_spec((1, C_WIDTH)), _const_spec((D_MODEL, N_BRANCH * D_MODEL)), _const_spec((512, D_MODEL)),
            _const_spec((512, D_MODEL)), _const_spec((512, D_MODEL)), _const_spec((D_MODEL, D_MODEL))],
        out_specs=wide,
        out_shape=jax.ShapeDtypeStruct((t, D_MODEL), F32),
        scratch_shapes=[pltpu.VMEM((16, tm, LANES), F32)],
        compiler_params=_params("parallel"),
    )(x2, n1, o0.reshape(t, 512), l0.reshape(t, 512), o1, l1, o2, l2, o_b, o_cf, o_cb, z, ogain, wg, wba, wbb, wbc, wo)


def _ffn_kernel(x_ref, n2_ref, wi_ref, wo_ref, out_ref):
    x = x_ref[...]
    h = _rms_rows(x, n2_ref[...]).astype(BF16)
    acc = x
    for c in range(0, FFN_HIDDEN, MXU):
        gate = _dot(h, wi_ref[:, c:c + MXU])
        up = _dot(h, wi_ref[:, FFN_HIDDEN + c:FFN_HIDDEN + c + MXU])
        act = (gate * _sigmoid(gate) * up).astype(BF16)
        acc = acc + _dot(act, wo_ref[c:c + MXU, :])
    out_ref[...] = acc


def _ffn(x2, n2, wi, wo):
    tm = TOKEN_TILE
    t = x2.shape[0]
    row = lambda i: (i, 0)
    return pl.pallas_call(
        _ffn_kernel,
        name="ffn",
        grid=(t // tm,),
        in_specs=[pl.BlockSpec((tm, D_MODEL), row), _const_spec((1, D_MODEL)),
                  _const_spec((D_MODEL, 2 * FFN_HIDDEN)), _const_spec((FFN_HIDDEN, D_MODEL))],
        out_specs=pl.BlockSpec((tm, D_MODEL), row),
        out_shape=jax.ShapeDtypeStruct((t, D_MODEL), F32),
        compiler_params=_params("parallel"),
    )(x2, n2, wi, wo)


def _rope_tables(seq):
    pos = jnp.arange(seq, dtype=F32)
    inv_freq = jnp.power(jnp.float32(ROPE_THETA), -jnp.arange(0, ROPE_DIM, 2, dtype=F32) / ROPE_DIM)
    ang = pos[:, None] * inv_freq[None, :]
    cos, sin = jnp.cos(ang), jnp.sin(ang)
    half = ROPE_DIM // 2
    one = jnp.ones((seq, HEAD_DIM - ROPE_DIM), F32)
    zero = jnp.zeros((seq, HEAD_DIM - half), F32)
    cos_h = jnp.concatenate([cos, cos, one], axis=1)
    sa_h = jnp.concatenate([-sin, zero], axis=1)
    sb_h = jnp.concatenate([jnp.zeros((seq, half), F32), sin, zero[:, half:]], axis=1)
    tile = lambda t: jnp.concatenate([t, t], axis=1)
    return tile(cos_h), tile(sa_h), tile(sb_h)


def _pack_layer(l, norm1, w_in, qk_gain, sink, conv_w, a_log, dt_bias, o_gain, w_gate, w_br_a, w_br_b, w_br_c,
                w_out, norm2, w_ffn_in, w_ffn_out):
    wi = w_in[l]
    kb0, kb1 = wi[:, 5120:5184], wi[:, 5184:5248]
    vb0, vb1 = wi[:, 5248:5312], wi[:, 5312:5376]
    w1 = jnp.concatenate([
        wi[:, :5120], kb0, kb0, kb1, kb1, vb0, vb0, vb1, vb1, wi[:, 5376:7424],
        wi[:, 7424:7456], jnp.zeros((D_MODEL, K1_BA - 32), F32)], axis=1).astype(BF16)
    scale = HEAD_DIM ** -0.5
    gains = jnp.stack([qk_gain[l, 0] * scale, qk_gain[l, 1], qk_gain[l, 2] * scale, qk_gain[l, 3]])
    gains = jnp.concatenate([gains, gains], axis=1)
    pad = lambda v: jnp.concatenate([jnp.zeros((16,), F32), v.reshape(-1), jnp.zeros((LANES - 32,), F32)])[None, :]
    return dict(
        n1=norm1[l][None, :], w1=w1, gains=gains, sink=sink[l], conv_w=conv_w[l],
        alog=pad(a_log[l]), dtb=pad(dt_bias[l]), ogain=jnp.tile(o_gain[l], C_HEADS)[None, :],
        wg=w_gate[l].astype(BF16), wba=w_br_a[l].astype(BF16), wbb=w_br_b[l].astype(BF16),
        wbc=w_br_c[l].astype(BF16), wo=w_out[l].astype(BF16), n2=norm2[l][None, :],
        wfi=w_ffn_in[l].astype(BF16), wfo=w_ffn_out[l].astype(BF16))


def _layer(x2, p, tabs, batch, seq):
    t = batch * seq
    a0, a1, a2, qkv_b, xc, z, ba = _inproj(x2, p["n1"], p["w1"], *tabs, p["gains"], batch, seq)
    a_res = _mixer_a((a0.reshape(batch, 1, seq, A_QKV), a1, a2))
    o_b = _mixer_b(qkv_b, p["sink"], batch, seq)
    factors = _delta_chunks(xc.reshape(batch, seq, K1_C), ba.reshape(batch, seq, K1_BA), p["conv_w"], p["alog"],
                            p["dtb"])
    o_cf, o_cb = _delta_scan(factors)
    x2 = _merge(x2, p["n1"], a_res, o_b, o_cf.reshape(t, C_WIDTH), o_cb.reshape(t, C_WIDTH), z, p["ogain"],
                p["wg"], p["wba"], p["wbb"], p["wbc"], p["wo"], seq)
    return _ffn(x2, p["n2"], p["wfi"], p["wfo"])


def kernel(x_prompt, x_sample, norm1, w_in, qk_gain, sink, conv_w, a_log, dt_bias, o_gain, w_gate, w_br_a, w_br_b,
           w_br_c, w_out, norm2, w_ffn_in, w_ffn_out):
    depth = w_in.shape[0]
    layers = [_pack_layer(l, norm1, w_in, qk_gain, sink, conv_w, a_log, dt_bias, o_gain, w_gate, w_br_a, w_br_b,
                          w_br_c, w_out, norm2, w_ffn_in, w_ffn_out) for l in range(depth)]
    outs = []
    for x in (x_prompt, x_sample):
        batch, seq, _ = x.shape
        assert seq % (A_DILATIONS[-1] * 256) == 0
        tabs = _rope_tables(seq)
        x2 = x.reshape(batch * seq, D_MODEL)
        for p in layers:
            x2 = _layer(x2, p, tabs, batch, seq)
        outs.append(x2.reshape(batch, seq, D_MODEL))
    return tuple(outs)
```

```python
import functools

import jax
import jax.numpy as jnp
from jax import lax
from jax.experimental import pallas as pl
from jax.experimental.pallas import tpu as pltpu

F32 = jnp.float32
BF16 = jnp.bfloat16

D_MODEL = 1024
HEAD_DIM = 64
ROPE_DIM = HEAD_DIM // 4
ROPE_THETA = 500000.0
EPS = 1e-6
NEG_INF = -1e30

A_PATTERNS = ((128, 1), (512, 4), (2048, 16))
A_DILATIONS = tuple(d for _, d in A_PATTERNS)
A_GROUPS = 3
A_WIDTH = 512
A_QKV = A_GROUPS * A_WIDTH
A_HALO = 64
B_HALF_WINDOW = 128
C_HEADS = 8
C_CHUNK = 64
C_CONV = 5
C_WIDTH = 512
FFN_HIDDEN = 2816
N_BRANCH = 3

LANES = 128
MXU = 256
VMEM_LIMIT = 56 * 1024 * 1024
TOKEN_TILE = 512

K1_A = 3 * A_QKV
K1_B = 512 + 256 + 256
K1_C = 3 * C_WIDTH
K1_Z = C_WIDTH
K1_BA = LANES
K1_OFF_B = K1_A
K1_OFF_C = K1_OFF_B + K1_B
K1_OFF_Z = K1_OFF_C + K1_C
K1_OFF_BA = K1_OFF_Z + K1_Z
K1_WIDTH = K1_OFF_BA + K1_BA


def _dot(a, b):
    return jnp.dot(a, b, preferred_element_type=F32)


def _dot_nt(a, b):
    return lax.dot_general(a, b, (((1,), (1,)), ((), ())), preferred_element_type=F32)


def _dot_tn(a, b):
    return lax.dot_general(a, b, (((0,), (0,)), ((), ())), preferred_element_type=F32)


def _split3(a):
    a1 = a.astype(BF16)
    r1 = a - a1.astype(F32)
    a2 = r1.astype(BF16)
    a3 = (r1 - a2.astype(F32)).astype(BF16)
    return a1, a2, a3


def _dot_data_sel(a, sel):
    a1, a2, a3 = _split3(a)
    return _dot(a1, sel) + _dot(a2, sel) + _dot(a3, sel)


def _dot_sel_data(sel, b):
    b1, b2, b3 = _split3(b)
    return _dot(sel, b1) + _dot(sel, b2) + _dot(sel, b3)


def _rms_rows(x, gain):
    ms = jnp.mean(x * x, axis=-1, keepdims=True)
    return x * lax.rsqrt(ms + EPS) * gain


def _sigmoid(x):
    return 1.0 / (1.0 + jnp.exp(-x))


def _head_ones(n):
    r = lax.broadcasted_iota(jnp.int32, (n, n), 0) // HEAD_DIM
    c = lax.broadcasted_iota(jnp.int32, (n, n), 1) // HEAD_DIM
    return jnp.where(r == c, 1.0, 0.0).astype(BF16)


def _const_spec(shape):
    nd = len(shape)
    return pl.BlockSpec(shape, lambda *_: (0,) * nd, pipeline_mode=pl.Buffered(1))


def _params(*sem):
    return pltpu.CompilerParams(dimension_semantics=sem, vmem_limit_bytes=VMEM_LIMIT)


def _inproj_kernel(x_ref, n1_ref, w_ref, cos_ref, sa_ref, sb_ref, gain_ref,
                   a0_ref, a1_ref, a2_ref, ob_ref, oc_ref, oz_ref, oba_ref, deint_scr):
    tm = x_ref.shape[0]
    h = _rms_rows(x_ref[...], n1_ref[...]).astype(BF16)
    ones = _head_ones(MXU)
    cos, sa, sb = cos_ref[...], sa_ref[...], sb_ref[...]
    a_refs = (a0_ref, a1_ref, a2_ref)
    slot = [0]

    def qk_norm_rope(acc, gain):
        ss = _dot((acc * acc).astype(BF16), ones)
        r = lax.rsqrt(ss * (1.0 / HEAD_DIM) + EPS)
        outs = []
        for s in range(0, MXU, LANES):
            y = acc[:, s:s + LANES] * r[:, s:s + LANES] * gain
            outs.append(y * cos + pltpu.roll(y, LANES - 8, 1) * sa + pltpu.roll(y, 8, 1) * sb)
        return outs

    def emit_a(g, col, val):
        d = A_DILATIONS[g]
        if d == 1:
            a0_ref[:, col:col + LANES] = val.astype(BF16)
            return
        s = slot[0]
        slot[0] = (s + 1) % deint_scr.shape[0]
        deint_scr[s] = val
        for r in range(d):
            a_refs[g][0, r, :, col:col + LANES] = deint_scr[s, pl.ds(r, tm // d, stride=d), :].astype(BF16)

    items = []

    def a_item(kind, g, c):
        def finish(acc):
            parts = qk_norm_rope(acc, gain_ref[kind:kind + 1, :]) if kind < 2 else [acc[:, :LANES], acc[:, LANES:]]
            for i, part in enumerate(parts):
                emit_a(g, kind * A_WIDTH + c + i * LANES, part)
        items.append((kind * A_QKV + g * A_WIDTH + c, MXU, finish))

    def qk_item(c0, gain_row, out_ref, o0):
        def finish(acc):
            for i, part in enumerate(qk_norm_rope(acc, gain_ref[gain_row:gain_row + 1, :])):
                out_ref[:, o0 + i * LANES:o0 + (i + 1) * LANES] = part.astype(out_ref.dtype)
        items.append((c0, MXU, finish))

    def plain_item(c0, width, out_ref, o0):
        def finish(acc):
            out_ref[:, o0:o0 + width] = acc.astype(out_ref.dtype)
        items.append((c0, width, finish))

    for kind in range(3):
        for g in range(A_GROUPS):
            for c in range(0, A_WIDTH, MXU):
                a_item(kind, g, c)
    for c in range(0, 512, MXU):
        qk_item(K1_OFF_B + c, 2, ob_ref, c)
    qk_item(K1_OFF_B + 512, 3, ob_ref, 512)
    plain_item(K1_OFF_B + 768, 256, ob_ref, 768)
    for c in range(0, K1_C, MXU):
        plain_item(K1_OFF_C + c, MXU, oc_ref, c)
    for c in range(0, K1_Z, MXU):
        plain_item(K1_OFF_Z + c, MXU, oz_ref, c)
    plain_item(K1_OFF_BA, K1_BA, oba_ref, 0)

    product = lambda it: _dot(h, w_ref[:, it[0]:it[0] + it[1]])
    acc = product(items[0])
    for i, it in enumerate(items):
        nxt = product(items[i + 1]) if i + 1 < len(items) else None
        it[2](acc)
        acc = nxt


def _inproj(x2, n1, w, cos_t, sa_t, sb_t, gains, batch, seq):
    tm = TOKEN_TILE
    t = x2.shape[0]
    ns = seq // tm
    row = lambda i: (i, 0)
    pos = lambda i: (i % ns, 0)
    d1, d2 = A_DILATIONS[1], A_DILATIONS[2]
    res = lambda i: (i // ns, 0, i % ns, 0)
    return pl.pallas_call(
        _inproj_kernel,
        name="inproj",
        grid=(t // tm,),
        in_specs=[
            pl.BlockSpec((tm, D_MODEL), row),
            _const_spec((1, D_MODEL)),
            _const_spec((D_MODEL, K1_WIDTH)),
            pl.BlockSpec((tm, LANES), pos),
            pl.BlockSpec((tm, LANES), pos),
            pl.BlockSpec((tm, LANES), pos),
            _const_spec((4, LANES)),
        ],
        out_specs=[
            pl.BlockSpec((tm, A_QKV), row),
            pl.BlockSpec((1, d1, tm // d1, A_QKV), res),
            pl.BlockSpec((1, d2, tm // d2, A_QKV), res),
            pl.BlockSpec((tm, K1_B), row),
            pl.BlockSpec((tm, K1_C), row),
            pl.BlockSpec((tm, K1_Z), row),
            pl.BlockSpec((tm, K1_BA), row),
        ],
        out_shape=[
            jax.ShapeDtypeStruct((t, A_QKV), BF16),
            jax.ShapeDtypeStruct((batch, d1, seq // d1, A_QKV), BF16),
            jax.ShapeDtypeStruct((batch, d2, seq // d2, A_QKV), BF16),
            jax.ShapeDtypeStruct((t, K1_B), BF16),
            jax.ShapeDtypeStruct((t, K1_C), F32),
            jax.ShapeDtypeStruct((t, K1_Z), F32),
            jax.ShapeDtypeStruct((t, K1_BA), F32),
        ],
        scratch_shapes=[pltpu.VMEM((4, tm, LANES), F32)],
        compiler_params=_params("parallel"),
    )(x2, n1, w, cos_t, sa_t, sb_t, gains)


def _attn_kernel(*refs, tq, halo, sub, length, kv_shared, has_sink, has_lse):
    refs = list(refs)
    sink_ref = refs.pop(0) if has_sink else None
    q_ref, kp_ref, kc_ref, kn_ref, vp_ref, vc_ref, vn_ref, o_ref = refs[:8]
    rest = refs[8:]
    lse_ref = rest.pop(0) if has_lse else None
    kbuf, vbuf = rest
    n = pl.program_id(2)
    nk = sub + 2 * halo

    kbuf[0:halo, :] = kp_ref[0, 0]
    kbuf[halo:halo + tq, :] = kc_ref[0, 0]
    kbuf[halo + tq:, :] = kn_ref[0, 0]
    vbuf[0:halo, :] = vp_ref[0, 0]
    vbuf[halo:halo + tq, :] = vc_ref[0, 0]
    vbuf[halo + tq:, :] = vn_ref[0, 0]

    rows = lax.broadcasted_iota(jnp.int32, (sub, nk), 0)
    cols = lax.broadcasted_iota(jnp.int32, (sub, nk), 1)
    rel = cols - halo - rows
    band = (rel <= halo) & (rel >= -halo)
    low_lane = lax.broadcasted_iota(jnp.int32, (sub, LANES), 1) < HEAD_DIM

    for s in range(tq // sub):
        kpos = n * tq + (s * sub - halo) + cols
        mask = band & (kpos >= 0) & (kpos < length)
        qrows = slice(s * sub, (s + 1) * sub)
        krows = slice(s * sub, s * sub + nk)
        heads = [(hp, half) for hp in range(4) for half in range(2)]
        kcol = [(hp // 2 if kv_shared else hp) * LANES for hp, _ in heads]
        sc = []
        for (hp, half), kc0 in zip(heads, kcol):
            qp = q_ref[0, 0, qrows, hp * LANES:(hp + 1) * LANES]
            qm = jnp.where(low_lane == (half == 0), qp, jnp.zeros_like(qp))
            sc.append(jnp.where(mask, _dot_nt(qm, kbuf[krows, kc0:kc0 + LANES]), NEG_INF))
        m = [jnp.max(x, axis=-1, keepdims=True) for x in sc]
        if has_sink:
            snk = [sink_ref[hp * 2 + half] for hp, half in heads]
            m = [jnp.maximum(x, y) for x, y in zip(m, snk)]
        p = [jnp.exp(x - y) for x, y in zip(sc, m)]
        den = [jnp.sum(x, axis=-1, keepdims=True) for x in p]
        if has_sink:
            den = [x + jnp.exp(y - z) for x, y, z in zip(den, snk, m)]
        o = [_dot(x.astype(BF16), vbuf[krows, kc0:kc0 + LANES]) / y for x, kc0, y in zip(p, kcol, den)]
        for hp in range(4):
            lanes = slice(hp * LANES, (hp + 1) * LANES)
            o_ref[0, 0, qrows, lanes] = jnp.where(low_lane, o[2 * hp], o[2 * hp + 1]).astype(o_ref.dtype)
            if has_lse:
                lse = [jnp.broadcast_to(m[i] + jnp.log(den[i]), (sub, LANES)) for i in (2 * hp, 2 * hp + 1)]
                lse_ref[0, 0, qrows, lanes] = jnp.where(low_lane, lse[0], lse[1])


def _banded_attention(qkv, *, name, q_cb, k_cb, v_cb, kv_width, halo, tq, sub, kv_shared, sink=None,
                      has_lse=False, out_dtype=F32):
    batch, nres, length, _ = qkv.shape
    nb = length // tq
    per = tq // halo
    last = length // halo - 1
    q_spec = pl.BlockSpec((1, 1, tq, 512), lambda b, r, n: (b, r, n, q_cb))

    def kv_specs(cb):
        return [
            pl.BlockSpec((1, 1, halo, kv_width), lambda b, r, n: (b, r, jnp.maximum(n * per - 1, 0), cb)),
            pl.BlockSpec((1, 1, tq, kv_width), lambda b, r, n: (b, r, n, cb)),
            pl.BlockSpec((1, 1, halo, kv_width), lambda b, r, n: (b, r, jnp.minimum((n + 1) * per, last), cb)),
        ]

    in_specs = [q_spec] + kv_specs(k_cb) + kv_specs(v_cb)
    args = [qkv] * 7
    if sink is not None:
        in_specs = [pl.BlockSpec(memory_space=pltpu.SMEM)] + in_specs
        args = [sink] + args
    o_spec = pl.BlockSpec((1, 1, tq, 512), lambda b, r, n: (b, r, n, 0))
    out_specs = [o_spec]
    out_shape = [jax.ShapeDtypeStruct((batch, nres, length, 512), out_dtype)]
    if has_lse:
        out_specs.append(o_spec)
        out_shape.append(jax.ShapeDtypeStruct((batch, nres, length, 512), F32))
    kern = functools.partial(_attn_kernel, tq=tq, halo=halo, sub=sub, length=length, kv_shared=kv_shared,
                             has_sink=sink is not None, has_lse=has_lse)
    return pl.pallas_call(
        kern,
        name=name,
        grid=(batch, nres, nb),
        in_specs=in_specs,
        out_specs=out_specs,
        out_shape=out_shape,
        scratch_shapes=[pltpu.VMEM((tq + 2 * halo, kv_width), BF16), pltpu.VMEM((tq + 2 * halo, kv_width), BF16)],
        compiler_params=_params("parallel", "parallel", "parallel"),
    )(*args)


def _mixer_a(groups):
    res = []
    for g, qkv in enumerate(groups):
        res.append(_banded_attention(qkv, name=f"attn_a{g}", q_cb=0, k_cb=1, v_cb=2, kv_width=512, halo=A_HALO,
                                     tq=256, sub=128, kv_shared=False, has_lse=True))
    return res


def _mixer_b(qkv_b, sink, batch, seq):
    (o,) = _banded_attention(
        qkv_b.reshape(batch, 1, seq, K1_B), name="attn_b", q_cb=0, k_cb=2, v_cb=3, kv_width=256,
        halo=B_HALF_WINDOW, tq=512, sub=128, kv_shared=True, sink=sink, out_dtype=BF16)
    return o.reshape(batch * seq, 512)


C_TILE = 256
C_HALO = 8
GROUP = 256
N_CHUNK = C_TILE // C_CHUNK


def _head_mask():
    r = lax.broadcasted_iota(jnp.int32, (GROUP, GROUP), 0) // HEAD_DIM
    c = lax.broadcasted_iota(jnp.int32, (GROUP, GROUP), 1) // HEAD_DIM
    return r == c


def _blockdiag(x, headmask):
    xb = x.astype(BF16)
    return jnp.where(headmask, jnp.concatenate([xb, xb, xb, xb], axis=0), jnp.zeros((), BF16))


def _delta_chunks_kernel(xp_ref, xc_ref, xn_ref, ba_ref, convw_ref, alog_ref, dtb_ref,
                         uf_ref, wf_ref, qkf_ref, qgf_ref, kdf_ref, glf_ref,
                         ub_ref, wb_ref, qkb_ref, qgb_ref, kdb_ref, glb_ref,
                         ext_scr, q_scr, k_scr, v_scr, beta_scr, gi_scr):
    n = pl.program_id(1)
    nt = pl.num_programs(1)
    outs = ((uf_ref, wf_ref, qkf_ref, qgf_ref, kdf_ref, glf_ref), (ub_ref, wb_ref, qkb_ref, qgb_ref, kdb_ref, glb_ref))

    ext_scr[0:C_HALO, :] = jnp.where(n > 0, xp_ref[0], 0.0)
    ext_scr[C_HALO:C_HALO + C_TILE, :] = xc_ref[0]
    ext_scr[C_HALO + C_TILE:, :] = jnp.where(n < nt - 1, xn_ref[0], 0.0)
    acc = None
    for tap in range(C_CONV):
        off = C_HALO - C_CONV // 2 + tap
        term = ext_scr[off:off + C_TILE, :] * convw_ref[tap:tap + 1, :]
        acc = term if acc is None else acc + term
    y = acc * _sigmoid(acc)
    ones = _head_ones(GROUP)
    for gi in range(2):
        sl = slice(gi * GROUP, (gi + 1) * GROUP)
        q = y[:, sl]
        k = y[:, 512 + gi * GROUP:512 + (gi + 1) * GROUP]
        ssq = _dot((q * q).astype(BF16), ones)
        ssk = _dot((k * k).astype(BF16), ones)
        q_scr[:, sl] = q * lax.rsqrt(ssq + EPS) * (HEAD_DIM ** -0.5)
        k_scr[:, sl] = k * lax.rsqrt(ssk + EPS)
    v_scr[...] = y[:, 1024:]

    ba = ba_ref[0]
    beta_all = _sigmoid(ba)
    zz = ba + dtb_ref[...]
    g_all = -jnp.exp(alog_ref[...]) * (jnp.maximum(zz, 0.0) + jnp.log(1.0 + jnp.exp(-jnp.abs(zz))))
    lane_head = lax.broadcasted_iota(jnp.int32, (LANES, 512), 1) // HEAD_DIM
    krow = lax.broadcasted_iota(jnp.int32, (LANES, 512), 0)
    ti = lax.broadcasted_iota(jnp.int32, (C_TILE, C_TILE), 0)
    tj = lax.broadcasted_iota(jnp.int32, (C_TILE, C_TILE), 1)
    same_chunk = (ti // C_CHUNK) == (tj // C_CHUNK)
    for d in range(2):
        e_beta = jnp.where(krow == lane_head + d * C_HEADS, 1.0, 0.0).astype(BF16)
        e_g = jnp.where(krow == lane_head + (2 + d) * C_HEADS, 1.0, 0.0).astype(BF16)
        beta_scr[d] = _dot_data_sel(beta_all, e_beta)
        g_rep = _dot_data_sel(g_all, e_g)
        tri = jnp.where(same_chunk & ((tj <= ti) if d == 0 else (tj >= ti)), 1.0, 0.0).astype(BF16)
        gi_scr[d] = _dot_sel_data(tri, g_rep)

    ii = lax.broadcasted_iota(jnp.int32, (C_CHUNK, 512), 0)
    jj = lax.broadcasted_iota(jnp.int32, (C_CHUNK, 512), 1) % C_CHUNK
    ig = lax.broadcasted_iota(jnp.int32, (C_CHUNK, GROUP), 0)
    jg = lax.broadcasted_iota(jnp.int32, (C_CHUNK, GROUP), 1) % C_CHUNK
    incl = (ii >= jj, ii <= jj)
    incl_g = (ig >= jg, ig <= jg)
    strict_g = (ig > jg, ig < jg)
    eye = jnp.where(ig == jg, 1.0, 0.0)
    diag16 = (ig // 16) == (jg // 16)
    headmask = _head_mask()
    zero_rows = jnp.zeros((LANES - C_CHUNK, LANES), F32)

    def bd(qs):
        return [_blockdiag(q, headmask) for q in qs]

    def mm(ps, bds):
        return [_dot(p.astype(BF16), b) for p, b in zip(ps, bds)]

    def mm2(ps, rs, bds):
        both = [_dot(jnp.concatenate([p, r], axis=0).astype(BF16), b) for p, r, b in zip(ps, rs, bds)]
        return [x[:C_CHUNK] for x in both], [x[C_CHUNK:] for x in both]

    def add(xs, ys):
        return [x + y for x, y in zip(xs, ys)]

    def unit_inverses(ls):
        dg = [jnp.where(diag16, l, 0.0) for l in ls]
        og = [l - d for l, d in zip(ls, dg)]
        di = [eye - d for d in dg]
        d2 = mm(dg, bd(dg))
        d4, t = mm2(d2, di, bd(d2))
        di = add(di, t)
        d8, t = mm2(d4, di, bd(d4))
        di = add(di, t)
        di = add(di, mm(di, bd(d8)))
        nn = mm(di, bd(og))
        m = [eye - x for x in nn]
        nbd = bd(nn)
        m = add(m, mm(mm(m, nbd), nbd))
        return mm(m, bd(di))

    def transposed_decay(gi_):
        parts = []
        for p in range(512 // LANES):
            t = jnp.concatenate([gi_[:, p * LANES:(p + 1) * LANES], zero_rows], axis=0).T
            parts.append(t[:C_CHUNK, :] + pltpu.roll(t[C_CHUNK:, :], HEAD_DIM, 1))
        return jnp.concatenate(parts, axis=1)

    def chunks(cs):
        lms, vbs, kbgs, dest = [], [], [], []
        for c in cs:
            rows = pl.ds(pl.multiple_of(c * C_CHUNK, C_CHUNK), C_CHUNK)
            qc, kc, vc = q_scr[rows, :], k_scr[rows, :], v_scr[rows, :]
            kk, qk = [], []
            for gidx in range(2):
                sl = slice(gidx * GROUP, (gidx + 1) * GROUP)
                kbd = _blockdiag(kc[:, sl], headmask)
                kk.append(_dot_nt(kc[:, sl].astype(BF16), kbd))
                qk.append(_dot_nt(qc[:, sl].astype(BF16), kbd))
            for d in range(2):
                u_ref, w_ref, qk_ref, qg_ref, kd_ref, gl_ref = outs[d]
                beta = beta_scr[d, rows, :]
                gi_ = gi_scr[d, rows, :]
                decay = jnp.exp(jnp.where(incl[d], gi_ - transposed_decay(gi_), NEG_INF))
                eg = jnp.exp(gi_)
                last = C_CHUNK - 1 if d == 0 else 0
                glrow = gi_[last:last + 1, :]
                kd_ref[0, rows, :] = (kc * jnp.exp(glrow - gi_)).astype(BF16)
                qg_ref[0, rows, :] = (qc * eg).astype(BF16)
                gl_ref[0, pl.ds(c, 1), 0, :] = jnp.exp(glrow)
                kbg = kc * beta * eg
                vb = vc * beta
                for gidx in range(2):
                    sl = slice(gidx * GROUP, (gidx + 1) * GROUP)
                    lms.append(jnp.where(strict_g[d], kk[gidx] * beta[:, sl] * decay[:, sl], 0.0))
                    qk_ref[0, rows, sl] = jnp.where(incl_g[d], qk[gidx] * decay[:, sl], 0.0).astype(BF16)
                    vbs.append(vb[:, sl])
                    kbgs.append(kbg[:, sl])
                    dest.append((u_ref, w_ref, rows, sl))
        tinvs = unit_inverses(lms)
        uw = mm(tinvs, [jnp.concatenate([a, b], axis=1) for a, b in zip(bd(vbs), bd(kbgs))])
        for (u_ref, w_ref, rows, sl), x in zip(dest, uw):
            u_ref[0, rows, sl] = x[:, :GROUP]
            w_ref[0, rows, sl] = x[:, GROUP:].astype(BF16)

    def body(c2, carry):
        chunks((2 * c2, 2 * c2 + 1))
        return carry

    lax.fori_loop(0, N_CHUNK // 2, body, 0)


def _delta_chunks(xc, ba, conv_w, alog_row, dtb_row):
    batch, seq, _ = xc.shape
    nt = seq // C_TILE
    per = C_TILE // C_HALO
    last = seq // C_HALO - 1
    tile = lambda b, n: (b, n, 0)
    wide = lambda dt: jax.ShapeDtypeStruct((batch, seq, C_WIDTH), dt)
    gl_sds = jax.ShapeDtypeStruct((batch, seq // C_CHUNK, 1, C_WIDTH), F32)
    out_shape = [wide(F32), wide(BF16), wide(BF16), wide(BF16), wide(BF16), gl_sds] * 2
    wide_spec = pl.BlockSpec((1, C_TILE, C_WIDTH), tile)
    gl_spec = pl.BlockSpec((1, N_CHUNK, 1, C_WIDTH), lambda b, n: (b, n, 0, 0))
    return pl.pallas_call(
        _delta_chunks_kernel,
        name="delta_chunks",
        grid=(batch, nt),
        in_specs=[
            pl.BlockSpec((1, C_HALO, K1_C), lambda b, n: (b, jnp.maximum(n * per - 1, 0), 0)),
            pl.BlockSpec((1, C_TILE, K1_C), tile),
            pl.BlockSpec((1, C_HALO, K1_C), lambda b, n: (b, jnp.minimum((n + 1) * per, last), 0)),
            pl.BlockSpec((1, C_TILE, K1_BA), tile),
            _const_spec((C_CONV, K1_C)), _const_spec((1, LANES)), _const_spec((1, LANES))],
        out_specs=[wide_spec] * 5 + [gl_spec] + [wide_spec] * 5 + [gl_spec],
        out_shape=out_shape,
        scratch_shapes=[
            pltpu.VMEM((C_TILE + 2 * C_HALO, K1_C), F32),
            pltpu.VMEM((C_TILE, C_WIDTH), F32),
            pltpu.VMEM((C_TILE, C_WIDTH), F32),
            pltpu.VMEM((C_TILE, C_WIDTH), F32),
            pltpu.VMEM((2, C_TILE, C_WIDTH), F32),
            pltpu.VMEM((2, C_TILE, C_WIDTH), F32),
        ],
        compiler_params=_params("parallel", "parallel"),
    )(xc, xc, xc, ba, conv_w, alog_row, dtb_row)


def _delta_scan_kernel(uf_ref, wf_ref, qkf_ref, qgf_ref, kdf_ref, glf_ref,
                       ub_ref, wb_ref, qkb_ref, qgb_ref, kdb_ref, glb_ref, of_ref, ob_ref, state_scr):
    n = pl.program_id(1)
    bblk = uf_ref.shape[0]
    ins = ((uf_ref, wf_ref, qkf_ref, qgf_ref, kdf_ref, glf_ref, of_ref),
           (ub_ref, wb_ref, qkb_ref, qgb_ref, kdb_ref, glb_ref, ob_ref))

    @pl.when(n == 0)
    def _():
        state_scr[...] = jnp.zeros_like(state_scr)

    headmask = _head_mask()

    def body(c, carry):
        chains = []
        for d in range(2):
            cc = c if d == 0 else N_CHUNK - 1 - c
            rows = pl.ds(pl.multiple_of(cc * C_CHUNK, C_CHUNK), C_CHUNK)
            for b in range(bblk):
                for gidx in range(2):
                    chains.append((d, b, gidx, cc, rows, slice(gidx * GROUP, (gidx + 1) * GROUP)))
        states = [state_scr[b, d, gidx] for d, b, gidx, _, _, _ in chains]
        both = [_dot(jnp.concatenate([ins[d][1][b, rows, sl], ins[d][3][b, rows, sl]], axis=0), st.astype(BF16))
                for (d, b, _, _, rows, sl), st in zip(chains, states)]
        v_new = [ins[d][0][b, rows, sl] - bo[:C_CHUNK] for (d, b, _, _, rows, sl), bo in zip(chains, both)]
        intra = [_dot(ins[d][2][b, rows, sl], _blockdiag(v, headmask)) for (d, b, _, _, rows, sl), v in zip(chains, v_new)]
        upd = [_dot_tn(ins[d][4][b, rows, sl], v.astype(BF16)) for (d, b, _, _, rows, sl), v in zip(chains, v_new)]
        for (d, b, gidx, cc, rows, sl), st, bo, it, up in zip(chains, states, both, intra, upd):
            ins[d][6][b, rows, sl] = bo[C_CHUNK:] + it
            gl = ins[d][5][b, pl.ds(cc, 1), 0, :]
            state_scr[b, d, gidx] = st * gl[:, sl] + jnp.where(headmask, up, 0.0)
        return carry

    lax.fori_loop(0, N_CHUNK, body, 0)


def _delta_scan(factors):
    batch, seq, _ = factors[0].shape
    nt = seq // C_TILE
    bblk = min(batch, 4)
    fwd = lambda b, n: (b, n, 0)
    bwd = lambda b, n: (b, nt - 1 - n, 0)

    def specs(tile):
        wide = pl.BlockSpec((bblk, C_TILE, C_WIDTH), tile)
        gl = pl.BlockSpec((bblk, N_CHUNK, 1, C_WIDTH), lambda b, n: tile(b, n) + (0,))
        return [wide] * 5 + [gl]

    out_sds = jax.ShapeDtypeStruct((batch, seq, C_WIDTH), F32)
    return pl.pallas_call(
        _delta_scan_kernel,
        name="delta_scan",
        grid=(batch // bblk, nt),
        in_specs=specs(fwd) + specs(bwd),
        out_specs=[pl.BlockSpec((bblk, C_TILE, C_WIDTH), fwd), pl.BlockSpec((bblk, C_TILE, C_WIDTH), bwd)],
        out_shape=[out_sds, out_sds],
        scratch_shapes=[pltpu.VMEM((bblk, 2, 2, GROUP, GROUP), F32)],
        compiler_params=_params("parallel", "arbitrary"),
    )(*factors)


def _merge_kernel(x_ref, n1_ref, oa0, la0, oa1, la1, oa2, la2, ob_ref, ocf_ref, ocb_ref, z_ref, ogain_ref,
                  wg_ref, wba_ref, wbb_ref, wbc_ref, wo_ref, out_ref, inter_scr):
    tm = x_ref.shape[0]
    x = x_ref[...]
    h = _rms_rows(x, n1_ref[...]).astype(BF16)

    def interleaved(ref, d, base):
        for r in range(d):
            for c in range(512 // LANES):
                inter_scr[base + c, pl.ds(r, tm // d, stride=d), :] = ref[0, r, :, c * LANES:(c + 1) * LANES]
        return jnp.concatenate([inter_scr[base + c] for c in range(512 // LANES)], axis=1)

    d1, d2 = A_DILATIONS[1], A_DILATIONS[2]
    l0, o0 = la0[...], oa0[...]
    l1, o1 = interleaved(la1, d1, 0), interleaved(oa1, d1, 4)
    l2, o2 = interleaved(la2, d2, 8), interleaved(oa2, d2, 12)
    m = jnp.maximum(jnp.maximum(l0, l1), l2)
    e0, e1, e2 = jnp.exp(l0 - m), jnp.exp(l1 - m), jnp.exp(l2 - m)
    o_a = (e0 * o0 + e1 * o1 + e2 * o2) / (e0 + e1 + e2)
    oc = ocf_ref[...] + ocb_ref[...]
    ones = _head_ones(MXU)
    z = z_ref[...]
    parts = []
    for c in range(0, C_WIDTH, MXU):
        blk = oc[:, c:c + MXU]
        ss = _dot((blk * blk).astype(BF16), ones)
        zz = z[:, c:c + MXU]
        parts.append(blk * lax.rsqrt(ss * (1.0 / HEAD_DIM) + EPS) * ogain_ref[:, c:c + MXU] * (zz * _sigmoid(zz)))
    o_c = jnp.concatenate(parts, axis=1)
    merged = None
    for br, (o_br, w_ref) in enumerate(((o_a, wba_ref), (ob_ref[...], wbb_ref), (o_c, wbc_ref))):
        gate = _sigmoid(_dot(h, wg_ref[:, br * D_MODEL:(br + 1) * D_MODEL]))
        term = gate * _dot(o_br.astype(BF16), w_ref[...])
        merged = term if merged is None else merged + term
    out_ref[...] = x + _dot(merged.astype(BF16), wo_ref[...])


def _merge(x2, n1, a_res, o_b, o_cf, o_cb, z, ogain, wg, wba, wbb, wbc, wo, seq):
    tm = TOKEN_TILE
    t = x2.shape[0]
    ns = seq // tm
    row = lambda i: (i, 0)
    res = lambda i: (i // ns, 0, i % ns, 0)
    wide = pl.BlockSpec((tm, D_MODEL), row)
    half = pl.BlockSpec((tm, 512), row)
    d1, d2 = A_DILATIONS[1], A_DILATIONS[2]
    res1 = pl.BlockSpec((1, d1, tm // d1, 512), res)
    res2 = pl.BlockSpec((1, d2, tm // d2, 512), res)
    (o0, l0), (o1, l1), (o2, l2) = a_res
    return pl.pallas_call(
        _merge_kernel,
        name="merge",
        grid=(t // tm,),
        in_specs=[wide, _const_spec((1, D_MODEL)), half, half, res1, res1, res2, res2] + [half] * 4 + [
            _const_spec((1, C_WIDTH)), _const_spec((D_MODEL, N_BRANCH * D_MODEL)), _const_spec((512, D_MODEL)),
            _const_spec((512, D_MODEL)), _const_spec((512, D_MODEL)), _const_spec((D_MODEL, D_MODEL))],
        out_specs=wide,
        out_shape=jax.ShapeDtypeStruct((t, D_MODEL), F32),
        scratch_shapes=[pltpu.VMEM((16, tm, LANES), F32)],
        compiler_params=_params("parallel"),
    )(x2, n1, o0.reshape(t, 512), l0.reshape(t, 512), o1, l1, o2, l2, o_b, o_cf, o_cb, z, ogain, wg, wba, wbb, wbc, wo)


def _ffn_kernel(x_ref, n2_ref, wi_ref, wo_ref, out_ref):
    x = x_ref[...]
    h = _rms_rows(x, n2_ref[...]).astype(BF16)
    acc = x
    for c in range(0, FFN_HIDDEN, MXU):
        gate = _dot(h, wi_ref[:, c:c + MXU])
        up = _dot(h, wi_ref[:, FFN_HIDDEN + c:FFN_HIDDEN + c + MXU])
        act = (gate * _sigmoid(gate) * up).astype(BF16)
        acc = acc + _dot(act, wo_ref[c:c + MXU, :])
    out_ref[...] = acc


def _ffn(x2, n2, wi, wo):
    tm = TOKEN_TILE
    t = x2.shape[0]
    row = lambda i: (i, 0)
    return pl.pallas_call(
        _ffn_kernel,
        name="ffn",
        grid=(t // tm,),
        in_specs=[pl.BlockSpec((tm, D_MODEL), row), _const_spec((1, D_MODEL)),
                  _const_spec((D_MODEL, 2 * FFN_HIDDEN)), _const_spec((FFN_HIDDEN, D_MODEL))],
        out_specs=pl.BlockSpec((tm, D_MODEL), row),
        out_shape=jax.ShapeDtypeStruct((t, D_MODEL), F32),
        compiler_params=_params("parallel"),
    )(x2, n2, wi, wo)


def _rope_tables(seq):
    pos = jnp.arange(seq, dtype=F32)
    inv_freq = jnp.power(jnp.float32(ROPE_THETA), -jnp.arange(0, ROPE_DIM, 2, dtype=F32) / ROPE_DIM)
    ang = pos[:, None] * inv_freq[None, :]
    cos, sin = jnp.cos(ang), jnp.sin(ang)
    half = ROPE_DIM // 2
    one = jnp.ones((seq, HEAD_DIM - ROPE_DIM), F32)
    zero = jnp.zeros((seq, HEAD_DIM - half), F32)
    cos_h = jnp.concatenate([cos, cos, one], axis=1)
    sa_h = jnp.concatenate([-sin, zero], axis=1)
    sb_h = jnp.concatenate([jnp.zeros((seq, half), F32), sin, zero[:, half:]], axis=1)
    tile = lambda t: jnp.concatenate([t, t], axis=1)
    return tile(cos_h), tile(sa_h), tile(sb_h)


def _pack_layer(l, norm1, w_in, qk_gain, sink, conv_w, a_log, dt_bias, o_gain, w_gate, w_br_a, w_br_b, w_br_c,
                w_out, norm2, w_ffn_in, w_ffn_out):
    wi = w_in[l]
    kb0, kb1 = wi[:, 5120:5184], wi[:, 5184:5248]
    vb0, vb1 = wi[:, 5248:5312], wi[:, 5312:5376]
    w1 = jnp.concatenate([
        wi[:, :5120], kb0, kb0, kb1, kb1, vb0, vb0, vb1, vb1, wi[:, 5376:7424],
        wi[:, 7424:7456], jnp.zeros((D_MODEL, K1_BA - 32), F32)], axis=1).astype(BF16)
    scale = HEAD_DIM ** -0.5
    gains = jnp.stack([qk_gain[l, 0] * scale, qk_gain[l, 1], qk_gain[l, 2] * scale, qk_gain[l, 3]])
    gains = jnp.concatenate([gains, gains], axis=1)
    pad = lambda v: jnp.concatenate([jnp.zeros((16,), F32), v.reshape(-1), jnp.zeros((LANES - 32,), F32)])[None, :]
    return dict(
        n1=norm1[l][None, :], w1=w1, gains=gains, sink=sink[l], conv_w=conv_w[l],
        alog=pad(a_log[l]), dtb=pad(dt_bias[l]), ogain=jnp.tile(o_gain[l], C_HEADS)[None, :],
        wg=w_gate[l].astype(BF16), wba=w_br_a[l].astype(BF16), wbb=w_br_b[l].astype(BF16),
        wbc=w_br_c[l].astype(BF16), wo=w_out[l].astype(BF16), n2=norm2[l][None, :],
        wfi=w_ffn_in[l].astype(BF16), wfo=w_ffn_out[l].astype(BF16))


def _layer(x2, p, tabs, batch, seq):
    t = batch * seq
    a0, a1, a2, qkv_b, xc, z, ba = _inproj(x2, p["n1"], p["w1"], *tabs, p["gains"], batch, seq)
    a_res = _mixer_a((a0.reshape(batch, 1, seq, A_QKV), a1, a2))
    o_b = _mixer_b(qkv_b, p["sink"], batch, seq)
    factors = _delta_chunks(xc.reshape(batch, seq, K1_C), ba.reshape(batch, seq, K1_BA), p["conv_w"], p["alog"],
                            p["dtb"])
    o_cf, o_cb = _delta_scan(factors)
    x2 = _merge(x2, p["n1"], a_res, o_b, o_cf.reshape(t, C_WIDTH), o_cb.reshape(t, C_WIDTH), z, p["ogain"],
                p["wg"], p["wba"], p["wbb"], p["wbc"], p["wo"], seq)
    return _ffn(x2, p["n2"], p["wfi"], p["wfo"])


def kernel(x_prompt, x_sample, norm1, w_in, qk_gain, sink, conv_w, a_log, dt_bias, o_gain, w_gate, w_br_a, w_br_b,
           w_br_c, w_out, norm2, w_ffn_in, w_ffn_out):
    depth = w_in.shape[0]
    layers = [_pack_layer(l, norm1, w_in, qk_gain, sink, conv_w, a_log, dt_bias, o_gain, w_gate, w_br_a, w_br_b,
                          w_br_c, w_out, norm2, w_ffn_in, w_ffn_out) for l in range(depth)]
    outs = []
    for x in (x_prompt, x_sample):
        batch, seq, _ = x.shape
        assert seq % (A_DILATIONS[-1] * 256) == 0
        tabs = _rope_tables(seq)
        x2 = x.reshape(batch * seq, D_MODEL)
        for p in layers:
            x2 = _layer(x2, p, tabs, batch, seq)
        outs.append(x2.reshape(batch, seq, D_MODEL))
    return tuple(outs)
```

```python
import functools

import jax
import jax.numpy as jnp
from jax import lax
from jax.experimental import pallas as pl
from jax.experimental.pallas import tpu as pltpu

F32 = jnp.float32
BF16 = jnp.bfloat16

D_MODEL = 1024
HEAD_DIM = 64
ROPE_DIM = HEAD_DIM // 4
ROPE_THETA = 500000.0
EPS = 1e-6
NEG_INF = -1e30
LOG2_E = 1.4426950408889634

A_PATTERNS = ((128, 1), (512, 4), (2048, 16))
A_DILATIONS = tuple(d for _, d in A_PATTERNS)
A_GROUPS = 3
A_WIDTH = 512
A_QKV = A_GROUPS * A_WIDTH
A_HALO = 64
B_HALF_WINDOW = 128
C_HEADS = 8
C_CHUNK = 64
C_CONV = 5
C_WIDTH = 512
FFN_HIDDEN = 2816
N_BRANCH = 3

LANES = 128
MXU = 256
VMEM_LIMIT = 56 * 1024 * 1024
TOKEN_TILE = 512

K1_A = 3 * A_QKV
K1_B = 512 + 256 + 256
K1_C = 3 * C_WIDTH
K1_Z = C_WIDTH
K1_BA = LANES
K1_OFF_B = K1_A
K1_OFF_C = K1_OFF_B + K1_B
K1_OFF_Z = K1_OFF_C + K1_C
K1_OFF_BA = K1_OFF_Z + K1_Z
K1_WIDTH = K1_OFF_BA + K1_BA


def _dot(a, b):
    return jnp.dot(a, b, preferred_element_type=F32)


def _dot_nt(a, b):
    return lax.dot_general(a, b, (((1,), (1,)), ((), ())), preferred_element_type=F32)


def _dot_tn(a, b):
    return lax.dot_general(a, b, (((0,), (0,)), ((), ())), preferred_element_type=F32)


def _split3(a):
    a1 = a.astype(BF16)
    r1 = a - a1.astype(F32)
    a2 = r1.astype(BF16)
    a3 = (r1 - a2.astype(F32)).astype(BF16)
    return a1, a2, a3


def _dot_data_sel(a, sel):
    a1, a2, a3 = _split3(a)
    return _dot(a1, sel) + _dot(a2, sel) + _dot(a3, sel)


def _dot_sel_data(sel, b):
    b1, b2, b3 = _split3(b)
    return _dot(sel, b1) + _dot(sel, b2) + _dot(sel, b3)


def _rms_rows(x, gain):
    ms = jnp.mean(x * x, axis=-1, keepdims=True)
    return x * lax.rsqrt(ms + EPS) * gain


def _sigmoid(x):
    return 1.0 / (1.0 + jnp.exp(-x))


def _head_ones(n):
    r = lax.broadcasted_iota(jnp.int32, (n, n), 0) // HEAD_DIM
    c = lax.broadcasted_iota(jnp.int32, (n, n), 1) // HEAD_DIM
    return jnp.where(r == c, 1.0, 0.0).astype(BF16)


def _const_spec(shape):
    nd = len(shape)
    return pl.BlockSpec(shape, lambda *_: (0,) * nd, pipeline_mode=pl.Buffered(1))


def _params(*sem):
    return pltpu.CompilerParams(dimension_semantics=sem, vmem_limit_bytes=VMEM_LIMIT)


def _inproj_kernel(x_ref, n1_ref, w_ref, cos_ref, sa_ref, sb_ref, gain_ref,
                   a0_ref, a1_ref, a2_ref, ob_ref, oc_ref, oz_ref, oba_ref, deint_scr):
    tm = x_ref.shape[0]
    h = _rms_rows(x_ref[...], n1_ref[...]).astype(BF16)
    ones = _head_ones(MXU)
    cos, sa, sb = cos_ref[...], sa_ref[...], sb_ref[...]
    a_refs = (a0_ref, a1_ref, a2_ref)
    slot = [0]

    def qk_norm_rope(acc, gain):
        ss = _dot((acc * acc).astype(BF16), ones)
        r = lax.rsqrt(ss * (1.0 / HEAD_DIM) + EPS)
        outs = []
        for s in range(0, MXU, LANES):
            y = acc[:, s:s + LANES] * r[:, s:s + LANES] * gain
            outs.append(y * cos + pltpu.roll(y, LANES - 8, 1) * sa + pltpu.roll(y, 8, 1) * sb)
        return outs

    def emit_a(g, col, val):
        d = A_DILATIONS[g]
        if d == 1:
            a0_ref[:, col:col + LANES] = val.astype(BF16)
            return
        s = slot[0]
        slot[0] = (s + 1) % deint_scr.shape[0]
        deint_scr[s] = val
        for r in range(d):
            a_refs[g][0, r, :, col:col + LANES] = deint_scr[s, pl.ds(r, tm // d, stride=d), :].astype(BF16)

    items = []

    def a_item(kind, g, c):
        def finish(acc):
            parts = qk_norm_rope(acc, gain_ref[kind:kind + 1, :]) if kind < 2 else [acc[:, :LANES], acc[:, LANES:]]
            for i, part in enumerate(parts):
                emit_a(g, kind * A_WIDTH + c + i * LANES, part)
        items.append((kind * A_QKV + g * A_WIDTH + c, MXU, finish))

    def qk_item(c0, gain_row, out_ref, o0):
        def finish(acc):
            for i, part in enumerate(qk_norm_rope(acc, gain_ref[gain_row:gain_row + 1, :])):
                out_ref[:, o0 + i * LANES:o0 + (i + 1) * LANES] = part.astype(out_ref.dtype)
        items.append((c0, MXU, finish))

    def plain_item(c0, width, out_ref, o0):
        def finish(acc):
            out_ref[:, o0:o0 + width] = acc.astype(out_ref.dtype)
        items.append((c0, width, finish))

    for kind in range(3):
        for g in range(A_GROUPS):
            for c in range(0, A_WIDTH, MXU):
                a_item(kind, g, c)
    for c in range(0, 512, MXU):
        qk_item(K1_OFF_B + c, 2, ob_ref, c)
    qk_item(K1_OFF_B + 512, 3, ob_ref, 512)
    plain_item(K1_OFF_B + 768, 256, ob_ref, 768)
    for c in range(0, K1_C, MXU):
        plain_item(K1_OFF_C + c, MXU, oc_ref, c)
    for c in range(0, K1_Z, MXU):
        plain_item(K1_OFF_Z + c, MXU, oz_ref, c)
    plain_item(K1_OFF_BA, K1_BA, oba_ref, 0)

    product = lambda it: _dot(h, w_ref[:, it[0]:it[0] + it[1]])
    acc = product(items[0])
    for i, it in enumerate(items):
        nxt = product(items[i + 1]) if i + 1 < len(items) else None
        it[2](acc)
        acc = nxt


def _inproj(x2, n1, w, cos_t, sa_t, sb_t, gains, batch, seq):
    tm = TOKEN_TILE
    t = x2.shape[0]
    ns = seq // tm
    row = lambda i: (i, 0)
    pos = lambda i: (i % ns, 0)
    d1, d2 = A_DILATIONS[1], A_DILATIONS[2]
    res = lambda i: (i // ns, 0, i % ns, 0)
    return pl.pallas_call(
        _inproj_kernel,
        name="inproj",
        grid=(t // tm,),
        in_specs=[
            pl.BlockSpec((tm, D_MODEL), row),
            _const_spec((1, D_MODEL)),
            _const_spec((D_MODEL, K1_WIDTH)),
            pl.BlockSpec((tm, LANES), pos),
            pl.BlockSpec((tm, LANES), pos),
            pl.BlockSpec((tm, LANES), pos),
            _const_spec((4, LANES)),
        ],
        out_specs=[
            pl.BlockSpec((tm, A_QKV), row),
            pl.BlockSpec((1, d1, tm // d1, A_QKV), res),
            pl.BlockSpec((1, d2, tm // d2, A_QKV), res),
            pl.BlockSpec((tm, K1_B), row),
            pl.BlockSpec((tm, K1_C), row),
            pl.BlockSpec((tm, K1_Z), row),
            pl.BlockSpec((tm, K1_BA), row),
        ],
        out_shape=[
            jax.ShapeDtypeStruct((t, A_QKV), BF16),
            jax.ShapeDtypeStruct((batch, d1, seq // d1, A_QKV), BF16),
            jax.ShapeDtypeStruct((batch, d2, seq // d2, A_QKV), BF16),
            jax.ShapeDtypeStruct((t, K1_B), BF16),
            jax.ShapeDtypeStruct((t, K1_C), F32),
            jax.ShapeDtypeStruct((t, K1_Z), F32),
            jax.ShapeDtypeStruct((t, K1_BA), F32),
        ],
        scratch_shapes=[pltpu.VMEM((4, tm, LANES), F32)],
        compiler_params=_params("parallel"),
    )(x2, n1, w, cos_t, sa_t, sb_t, gains)


def _attn_kernel(*refs, tq, halo, sub, length, kv_shared, has_sink, has_lse):
    refs = list(refs)
    sink_ref = refs.pop(0) if has_sink else None
    q_ref, kp_ref, kc_ref, kn_ref, vp_ref, vc_ref, vn_ref, o_ref = refs[:8]
    rest = refs[8:]
    lse_ref = rest.pop(0) if has_lse else None
    kbuf, vbuf = rest
    n = pl.program_id(2)
    nk = sub + 2 * halo

    kbuf[0:halo, :] = kp_ref[0, 0]
    kbuf[halo:halo + tq, :] = kc_ref[0, 0]
    kbuf[halo + tq:, :] = kn_ref[0, 0]
    vbuf[0:halo, :] = vp_ref[0, 0]
    vbuf[halo:halo + tq, :] = vc_ref[0, 0]
    vbuf[halo + tq:, :] = vn_ref[0, 0]

    rows = lax.broadcasted_iota(jnp.int32, (2 * sub, nk), 0) % sub
    cols = lax.broadcasted_iota(jnp.int32, (2 * sub, nk), 1)
    rel = cols - halo - rows
    band = (rel <= halo) & (rel >= -halo)
    lane_low = lax.broadcasted_iota(jnp.int32, (2 * sub, LANES), 1) < HEAD_DIM
    row_first = lax.broadcasted_iota(jnp.int32, (2 * sub, LANES), 0) < sub
    own_lanes = lane_low == row_first
    low_lane = lax.broadcasted_iota(jnp.int32, (sub, LANES), 1) < HEAD_DIM
    first_row = lax.broadcasted_iota(jnp.int32, (2 * sub, 1), 0) < sub

    for s in range(tq // sub):
        kpos = n * tq + (s * sub - halo) + cols
        mask = band & (kpos >= 0) & (kpos < length)
        qrows = slice(s * sub, (s + 1) * sub)
        krows = slice(s * sub, s * sub + nk)
        kcol = [(hp // 2 if kv_shared else hp) * LANES for hp in range(4)]
        sc = []
        for hp in range(4):
            qp = q_ref[0, 0, qrows, hp * LANES:(hp + 1) * LANES]
            qm = jnp.where(own_lanes, jnp.concatenate([qp, qp], axis=0), jnp.zeros((), qp.dtype))
            sc.append(jnp.where(mask, _dot_nt(qm, kbuf[krows, kcol[hp]:kcol[hp] + LANES]), NEG_INF))
        m = [jnp.max(x, axis=-1, keepdims=True) for x in sc]
        if has_sink:
            snk = [jnp.where(first_row, sink_ref[2 * hp], sink_ref[2 * hp + 1]) for hp in range(4)]
            m = [jnp.maximum(x, y) for x, y in zip(m, snk)]
        p = [jnp.exp2(x - y) for x, y in zip(sc, m)]
        den = [jnp.sum(x, axis=-1, keepdims=True) for x in p]
        if has_sink:
            den = [x + jnp.exp2(y - z) for x, y, z in zip(den, snk, m)]
        o = [_dot(x.astype(BF16), vbuf[krows, kc0:kc0 + LANES]) / y for x, kc0, y in zip(p, kcol, den)]
        for hp in range(4):
            lanes = slice(hp * LANES, (hp + 1) * LANES)
            o_ref[0, 0, qrows, lanes] = jnp.where(low_lane, o[hp][:sub], o[hp][sub:]).astype(o_ref.dtype)
            if has_lse:
                lse = jnp.broadcast_to(m[hp] + jnp.log2(den[hp]), (2 * sub, LANES))
                lse_ref[0, 0, qrows, lanes] = jnp.where(low_lane, lse[:sub], lse[sub:])


def _banded_attention(qkv, *, name, q_cb, k_cb, v_cb, kv_width, halo, tq, sub, kv_shared, sink=None,
                      has_lse=False, out_dtype=F32):
    batch, nres, length, _ = qkv.shape
    nb = length // tq
    per = tq // halo
    last = length // halo - 1
    q_spec = pl.BlockSpec((1, 1, tq, 512), lambda b, r, n: (b, r, n, q_cb))

    def kv_specs(cb):
        return [
            pl.BlockSpec((1, 1, halo, kv_width), lambda b, r, n: (b, r, jnp.maximum(n * per - 1, 0), cb)),
            pl.BlockSpec((1, 1, tq, kv_width), lambda b, r, n: (b, r, n, cb)),
            pl.BlockSpec((1, 1, halo, kv_width), lambda b, r, n: (b, r, jnp.minimum((n + 1) * per, last), cb)),
        ]

    in_specs = [q_spec] + kv_specs(k_cb) + kv_specs(v_cb)
    args = [qkv] * 7
    if sink is not None:
        in_specs = [pl.BlockSpec(memory_space=pltpu.SMEM)] + in_specs
        args = [sink] + args
    o_spec = pl.BlockSpec((1, 1, tq, 512), lambda b, r, n: (b, r, n, 0))
    out_specs = [o_spec]
    out_shape = [jax.ShapeDtypeStruct((batch, nres, length, 512), out_dtype)]
    if has_lse:
        out_specs.append(o_spec)
        out_shape.append(jax.ShapeDtypeStruct((batch, nres, length, 512), F32))
    kern = functools.partial(_attn_kernel, tq=tq, halo=halo, sub=sub, length=length, kv_shared=kv_shared,
                             has_sink=sink is not None, has_lse=has_lse)
    return pl.pallas_call(
        kern,
        name=name,
        grid=(batch, nres, nb),
        in_specs=in_specs,
        out_specs=out_specs,
        out_shape=out_shape,
        scratch_shapes=[pltpu.VMEM((tq + 2 * halo, kv_width), BF16), pltpu.VMEM((tq + 2 * halo, kv_width), BF16)],
        compiler_params=_params("parallel", "parallel", "parallel"),
    )(*args)


def _mixer_a(groups):
    res = []
    for g, qkv in enumerate(groups):
        res.append(_banded_attention(qkv, name=f"attn_a{g}", q_cb=0, k_cb=1, v_cb=2, kv_width=512, halo=A_HALO,
                                     tq=min(512, qkv.shape[2]), sub=128, kv_shared=False, has_lse=True))
    return res


def _mixer_b(qkv_b, sink, batch, seq):
    (o,) = _banded_attention(
        qkv_b.reshape(batch, 1, seq, K1_B), name="attn_b", q_cb=0, k_cb=2, v_cb=3, kv_width=256,
        halo=B_HALF_WINDOW, tq=512, sub=128, kv_shared=True, sink=sink, out_dtype=BF16)
    return o.reshape(batch * seq, 512)


C_TILE = 512
SCAN_TILE = 256
C_HALO = 8
GROUP = 256
N_CHUNK = C_TILE // C_CHUNK
SCAN_CHUNKS = SCAN_TILE // C_CHUNK


def _head_mask():
    r = lax.broadcasted_iota(jnp.int32, (GROUP, GROUP), 0) // HEAD_DIM
    c = lax.broadcasted_iota(jnp.int32, (GROUP, GROUP), 1) // HEAD_DIM
    return r == c


def _blockdiag(x, headmask):
    xb = x.astype(BF16)
    return jnp.concatenate([xb, xb, xb, xb], axis=0) * headmask


def _delta_chunks_kernel(xp_ref, xc_ref, xn_ref, ba_ref, convw_ref, alog_ref, dtb_ref,
                         uf_ref, wf_ref, qkf_ref, qgf_ref, kdf_ref, glf_ref,
                         ub_ref, wb_ref, qkb_ref, qgb_ref, kdb_ref, glb_ref,
                         ext_scr, q_scr, k_scr, v_scr, beta_scr, gi_scr):
    n = pl.program_id(1)
    nt = pl.num_programs(1)
    outs = ((uf_ref, wf_ref, qkf_ref, qgf_ref, kdf_ref, glf_ref), (ub_ref, wb_ref, qkb_ref, qgb_ref, kdb_ref, glb_ref))

    ext_scr[0:C_HALO, :] = jnp.where(n > 0, xp_ref[0], 0.0)
    ext_scr[C_HALO:C_HALO + C_TILE, :] = xc_ref[0]
    ext_scr[C_HALO + C_TILE:, :] = jnp.where(n < nt - 1, xn_ref[0], 0.0)
    acc = None
    for tap in range(C_CONV):
        off = C_HALO - C_CONV // 2 + tap
        term = ext_scr[off:off + C_TILE, :] * convw_ref[tap:tap + 1, :]
        acc = term if acc is None else acc + term
    y = acc * _sigmoid(acc)
    ones = _head_ones(GROUP)
    for gi in range(2):
        sl = slice(gi * GROUP, (gi + 1) * GROUP)
        q = y[:, sl]
        k = y[:, 512 + gi * GROUP:512 + (gi + 1) * GROUP]
        ssq = _dot((q * q).astype(BF16), ones)
        ssk = _dot((k * k).astype(BF16), ones)
        q_scr[:, sl] = q * lax.rsqrt(ssq + EPS) * (HEAD_DIM ** -0.5)
        k_scr[:, sl] = k * lax.rsqrt(ssk + EPS)
    v_scr[...] = y[:, 1024:]

    ba = ba_ref[0]
    beta_all = _sigmoid(ba)
    zz = ba + dtb_ref[...]
    g_all = -jnp.exp(alog_ref[...]) * (jnp.maximum(zz, 0.0) + jnp.log(1.0 + jnp.exp(-jnp.abs(zz))))
    lane_head = lax.broadcasted_iota(jnp.int32, (LANES, 512), 1) // HEAD_DIM
    krow = lax.broadcasted_iota(jnp.int32, (LANES, 512), 0)
    ti = lax.broadcasted_iota(jnp.int32, (MXU, MXU), 0)
    tj = lax.broadcasted_iota(jnp.int32, (MXU, MXU), 1)
    same_chunk = (ti // C_CHUNK) == (tj // C_CHUNK)
    for d in range(2):
        e_beta = jnp.where(krow == lane_head + d * C_HEADS, 1.0, 0.0).astype(BF16)
        e_g = jnp.where(krow == lane_head + (2 + d) * C_HEADS, 1.0, 0.0).astype(BF16)
        beta_scr[d] = _dot_data_sel(beta_all, e_beta)
        tri = jnp.where(same_chunk & ((tj <= ti) if d == 0 else (tj >= ti)), 1.0, 0.0).astype(BF16)
        for r in range(0, C_TILE, MXU):
            gi_scr[d, r:r + MXU, :] = _dot_data_sel(_dot_sel_data(tri, g_all[r:r + MXU]), e_g)

    ii = lax.broadcasted_iota(jnp.int32, (C_CHUNK, 512), 0)
    jj = lax.broadcasted_iota(jnp.int32, (C_CHUNK, 512), 1) % C_CHUNK
    ig = lax.broadcasted_iota(jnp.int32, (C_CHUNK, GROUP), 0)
    jg = lax.broadcasted_iota(jnp.int32, (C_CHUNK, GROUP), 1) % C_CHUNK
    incl = (ii >= jj, ii <= jj)
    incl_g = (ig >= jg, ig <= jg)
    strict_g = (ig > jg, ig < jg)
    eye = jnp.where(ig == jg, 1.0, 0.0)
    diag16 = (ig // 16) == (jg // 16)
    headmask = ones
    zero_rows = jnp.zeros((LANES - C_CHUNK, LANES), F32)

    def bd(qs):
        return [_blockdiag(q, headmask) for q in qs]

    def mm(ps, bds):
        return [_dot(p.astype(BF16), b) for p, b in zip(ps, bds)]

    def mm2(ps, rs, bds):
        both = [_dot(jnp.concatenate([p, r], axis=0).astype(BF16), b) for p, r, b in zip(ps, rs, bds)]
        return [x[:C_CHUNK] for x in both], [x[C_CHUNK:] for x in both]

    def add(xs, ys):
        return [x + y for x, y in zip(xs, ys)]

    def unit_inverses(ls):
        dg = [jnp.where(diag16, l, 0.0) for l in ls]
        og = [l - d for l, d in zip(ls, dg)]
        di = [eye - d for d in dg]
        d2 = mm(dg, bd(dg))
        d4, t = mm2(d2, di, bd(d2))
        di = add(di, t)
        d8, t = mm2(d4, di, bd(d4))
        di = add(di, t)
        di = add(di, mm(di, bd(d8)))
        nn = mm(di, bd(og))
        m = [eye - x for x in nn]
        nbd = bd(nn)
        m = add(m, mm(mm(m, nbd), nbd))
        return mm(m, bd(di))

    def transposed_decay(gi_):
        parts = []
        for p in range(512 // LANES):
            t = jnp.concatenate([gi_[:, p * LANES:(p + 1) * LANES], zero_rows], axis=0).T
            parts.append(t[:C_CHUNK, :] + pltpu.roll(t[C_CHUNK:, :], HEAD_DIM, 1))
        return jnp.concatenate(parts, axis=1)

    def chunks(cs):
        lms, vbs, kbgs, dest = [], [], [], []
        for c in cs:
            rows = pl.ds(pl.multiple_of(c * C_CHUNK, C_CHUNK), C_CHUNK)
            qc, kc, vc = q_scr[rows, :], k_scr[rows, :], v_scr[rows, :]
            kk, qk = [], []
            for gidx in range(2):
                sl = slice(gidx * GROUP, (gidx + 1) * GROUP)
                kbd = _blockdiag(kc[:, sl], headmask)
                kk.append(_dot_nt(kc[:, sl].astype(BF16), kbd))
                qk.append(_dot_nt(qc[:, sl].astype(BF16), kbd))
            for d in range(2):
                u_ref, w_ref, qk_ref, qg_ref, kd_ref, gl_ref = outs[d]
                beta = beta_scr[d, rows, :]
                gi_ = gi_scr[d, rows, :]
                decay = jnp.exp(jnp.where(incl[d], gi_ - transposed_decay(gi_), NEG_INF))
                eg = jnp.exp(gi_)
                last = C_CHUNK - 1 if d == 0 else 0
                glrow = gi_[last:last + 1, :]
                kd_ref[0, rows, :] = (kc * jnp.exp(glrow - gi_)).astype(BF16)
                qg_ref[0, rows, :] = (qc * eg).astype(BF16)
                gl_ref[0, pl.ds(c, 1), 0, :] = jnp.exp(glrow)
                kbg = kc * beta * eg
                vb = vc * beta
                for gidx in range(2):
                    sl = slice(gidx * GROUP, (gidx + 1) * GROUP)
                    lms.append(jnp.where(strict_g[d], kk[gidx] * beta[:, sl] * decay[:, sl], 0.0))
                    qk_ref[0, rows, sl] = jnp.where(incl_g[d], qk[gidx] * decay[:, sl], 0.0).astype(BF16)
                    vbs.append(vb[:, sl])
                    kbgs.append(kbg[:, sl])
                    dest.append((u_ref, w_ref, rows, sl))
        tinvs = unit_inverses(lms)
        uw = mm(tinvs, [jnp.concatenate([a, b], axis=1) for a, b in zip(bd(vbs), bd(kbgs))])
        for (u_ref, w_ref, rows, sl), x in zip(dest, uw):
            u_ref[0, rows, sl] = x[:, :GROUP]
            w_ref[0, rows, sl] = x[:, GROUP:].astype(BF16)

    def body(c2, carry):
        chunks((2 * c2, 2 * c2 + 1))
        return carry

    lax.fori_loop(0, N_CHUNK // 2, body, 0)


def _delta_chunks(xc, ba, conv_w, alog_row, dtb_row):
    batch, seq, _ = xc.shape
    nt = seq // C_TILE
    per = C_TILE // C_HALO
    last = seq // C_HALO - 1
    tile = lambda b, n: (b, n, 0)
    wide = lambda dt: jax.ShapeDtypeStruct((batch, seq, C_WIDTH), dt)
    gl_sds = jax.ShapeDtypeStruct((batch, seq // C_CHUNK, 1, C_WIDTH), F32)
    out_shape = [wide(F32), wide(BF16), wide(BF16), wide(BF16), wide(BF16), gl_sds] * 2
    wide_spec = pl.BlockSpec((1, C_TILE, C_WIDTH), tile)
    gl_spec = pl.BlockSpec((1, N_CHUNK, 1, C_WIDTH), lambda b, n: (b, n, 0, 0))
    return pl.pallas_call(
        _delta_chunks_kernel,
        name="delta_chunks",
        grid=(batch, nt),
        in_specs=[
            pl.BlockSpec((1, C_HALO, K1_C), lambda b, n: (b, jnp.maximum(n * per - 1, 0), 0)),
            pl.BlockSpec((1, C_TILE, K1_C), tile),
            pl.BlockSpec((1, C_HALO, K1_C), lambda b, n: (b, jnp.minimum((n + 1) * per, last), 0)),
            pl.BlockSpec((1, C_TILE, K1_BA), tile),
            _const_spec((C_CONV, K1_C)), _const_spec((1, LANES)), _const_spec((1, LANES))],
        out_specs=[wide_spec] * 5 + [gl_spec] + [wide_spec] * 5 + [gl_spec],
        out_shape=out_shape,
        scratch_shapes=[
            pltpu.VMEM((C_TILE + 2 * C_HALO, K1_C), F32),
            pltpu.VMEM((C_TILE, C_WIDTH), F32),
            pltpu.VMEM((C_TILE, C_WIDTH), F32),
            pltpu.VMEM((C_TILE, C_WIDTH), F32),
            pltpu.VMEM((2, C_TILE, C_WIDTH), F32),
            pltpu.VMEM((2, C_TILE, C_WIDTH), F32),
        ],
        compiler_params=_params("parallel", "parallel"),
    )(xc, xc, xc, ba, conv_w, alog_row, dtb_row)


def _delta_scan_kernel(uf_ref, wf_ref, qkf_ref, qgf_ref, kdf_ref, glf_ref,
                       ub_ref, wb_ref, qkb_ref, qgb_ref, kdb_ref, glb_ref, of_ref, ob_ref, state_scr):
    n = pl.program_id(1)
    bblk = uf_ref.shape[0]
    ins = ((uf_ref, wf_ref, qkf_ref, qgf_ref, kdf_ref, glf_ref, of_ref),
           (ub_ref, wb_ref, qkb_ref, qgb_ref, kdb_ref, glb_ref, ob_ref))

    @pl.when(n == 0)
    def _():
        state_scr[...] = jnp.zeros_like(state_scr)

    headmask = _head_mask()
    head01 = _head_ones(GROUP)

    def body(c, carry):
        chains = []
        for d in range(2):
            cc = c if d == 0 else SCAN_CHUNKS - 1 - c
            rows = pl.ds(pl.multiple_of(cc * C_CHUNK, C_CHUNK), C_CHUNK)
            for b in range(bblk):
                for gidx in range(2):
                    chains.append((d, b, gidx, cc, rows, slice(gidx * GROUP, (gidx + 1) * GROUP)))
        states = [state_scr[b, d, gidx] for d, b, gidx, _, _, _ in chains]
        both = [_dot(jnp.concatenate([ins[d][1][b, rows, sl], ins[d][3][b, rows, sl]], axis=0), st.astype(BF16))
                for (d, b, _, _, rows, sl), st in zip(chains, states)]
        v_new = [ins[d][0][b, rows, sl] - bo[:C_CHUNK] for (d, b, _, _, rows, sl), bo in zip(chains, both)]
        intra = [_dot(ins[d][2][b, rows, sl], _blockdiag(v, head01)) for (d, b, _, _, rows, sl), v in zip(chains, v_new)]
        upd = [_dot_tn(ins[d][4][b, rows, sl], v.astype(BF16)) for (d, b, _, _, rows, sl), v in zip(chains, v_new)]
        for (d, b, gidx, cc, rows, sl), st, bo, it, up in zip(chains, states, both, intra, upd):
            ins[d][6][b, rows, sl] = bo[C_CHUNK:] + it
            gl = ins[d][5][b, pl.ds(cc, 1), 0, :]
            state_scr[b, d, gidx] = st * gl[:, sl] + jnp.where(headmask, up, 0.0)
        return carry

    lax.fori_loop(0, SCAN_CHUNKS, body, 0)


def _delta_scan(factors):
    batch, seq, _ = factors[0].shape
    nt = seq // SCAN_TILE
    bblk = min(batch, 4)
    fwd = lambda b, n: (b, n, 0)
    bwd = lambda b, n: (b, nt - 1 - n, 0)

    def specs(tile):
        wide = pl.BlockSpec((bblk, SCAN_TILE, C_WIDTH), tile)
        gl = pl.BlockSpec((bblk, SCAN_CHUNKS, 1, C_WIDTH), lambda b, n: tile(b, n) + (0,))
        return [wide] * 5 + [gl]

    out_sds = jax.ShapeDtypeStruct((batch, seq, C_WIDTH), F32)
    return pl.pallas_call(
        _delta_scan_kernel,
        name="delta_scan",
        grid=(batch // bblk, nt),
        in_specs=specs(fwd) + specs(bwd),
        out_specs=[pl.BlockSpec((bblk, SCAN_TILE, C_WIDTH), fwd), pl.BlockSpec((bblk, SCAN_TILE, C_WIDTH), bwd)],
        out_shape=[out_sds, out_sds],
        scratch_shapes=[pltpu.VMEM((bblk, 2, 2, GROUP, GROUP), F32)],
        compiler_params=_params("parallel", "arbitrary"),
    )(*factors)


def _merge_kernel(x_ref, n1_ref, oa0, la0, oa1, la1, oa2, la2, ob_ref, ocf_ref, ocb_ref, z_ref, ogain_ref,
                  wg_ref, wba_ref, wbb_ref, wbc_ref, wo_ref, out_ref, inter_scr):
    tm = x_ref.shape[0]
    x = x_ref[...]
    h = _rms_rows(x, n1_ref[...]).astype(BF16)

    def interleaved(ref, d, base):
        for r in range(d):
            for c in range(512 // LANES):
                inter_scr[base + c, pl.ds(r, tm // d, stride=d), :] = ref[0, r, :, c * LANES:(c + 1) * LANES]
        return jnp.concatenate([inter_scr[base + c] for c in range(512 // LANES)], axis=1)

    def branch(br, o_br, w_ref):
        gate = _sigmoid(_dot(h, wg_ref[:, br * D_MODEL:(br + 1) * D_MODEL]))
        return gate * _dot(o_br.astype(BF16), w_ref[...])

    term_b = branch(1, ob_ref[...], wbb_ref)

    d1, d2 = A_DILATIONS[1], A_DILATIONS[2]
    l0, o0 = la0[...], oa0[...]
    l1, o1 = interleaved(la1, d1, 0), interleaved(oa1, d1, 4)
    l2, o2 = interleaved(la2, d2, 8), interleaved(oa2, d2, 12)
    m = jnp.maximum(jnp.maximum(l0, l1), l2)
    e0, e1, e2 = jnp.exp2(l0 - m), jnp.exp2(l1 - m), jnp.exp2(l2 - m)
    o_a = (e0 * o0 + e1 * o1 + e2 * o2) / (e0 + e1 + e2)
    merged = branch(0, o_a, wba_ref) + term_b
    oc = ocf_ref[...] + ocb_ref[...]
    ones = _head_ones(MXU)
    z = z_ref[...]
    parts = []
    for c in range(0, C_WIDTH, MXU):
        blk = oc[:, c:c + MXU]
        ss = _dot((blk * blk).astype(BF16), ones)
        zz = z[:, c:c + MXU]
        parts.append(blk * lax.rsqrt(ss * (1.0 / HEAD_DIM) + EPS) * ogain_ref[:, c:c + MXU] * (zz * _sigmoid(zz)))
    o_c = jnp.concatenate(parts, axis=1)
    merged = merged + branch(2, o_c, wbc_ref)
    out_ref[...] = x + _dot(merged.astype(BF16), wo_ref[...])


def _merge(x2, n1, a_res, o_b, o_cf, o_cb, z, ogain, wg, wba, wbb, wbc, wo, seq):
    tm = TOKEN_TILE
    t = x2.shape[0]
    ns = seq // tm
    row = lambda i: (i, 0)
    res = lambda i: (i // ns, 0, i % ns, 0)
    wide = pl.BlockSpec((tm, D_MODEL), row)
    half = pl.BlockSpec((tm, 512), row)
    d1, d2 = A_DILATIONS[1], A_DILATIONS[2]
    res1 = pl.BlockSpec((1, d1, tm // d1, 512), res)
    res2 = pl.BlockSpec((1, d2, tm // d2, 512), res)
    (o0, l0), (o1, l1), (o2, l2) = a_res
    return pl.pallas_call(
        _merge_kernel,
        name="merge",
        grid=(t // tm,),
        in_specs=[wide, _const_spec((1, D_MODEL)), half, half, res1, res1, res2, res2] + [half] * 4 + [
            _const_spec((1, C_WIDTH)), _const_spec((D_MODEL, N_BRANCH * D_MODEL)), _const_spec((512, D_MODEL)),
            _const_spec((512, D_MODEL)), _const_spec((512, D_MODEL)), _const_spec((D_MODEL, D_MODEL))],
        out_specs=wide,
        out_shape=jax.ShapeDtypeStruct((t, D_MODEL), F32),
        scratch_shapes=[pltpu.VMEM((16, tm, LANES), F32)],
        compiler_params=_params("parallel"),
    )(x2, n1, o0.reshape(t, 512), l0.reshape(t, 512), o1, l1, o2, l2, o_b, o_cf, o_cb, z, ogain, wg, wba, wbb, wbc, wo)


def _ffn_kernel(x_ref, n2_ref, wi_ref, wo_ref, out_ref):
    x = x_ref[...]
    h = _rms_rows(x, n2_ref[...]).astype(BF16)
    acc = x
    for c in range(0, FFN_HIDDEN, MXU):
        gate = _dot(h, wi_ref[:, c:c + MXU])
        up = _dot(h, wi_ref[:, FFN_HIDDEN + c:FFN_HIDDEN + c + MXU])
        act = (gate * _sigmoid(gate) * up).astype(BF16)
        acc = acc + _dot(act, wo_ref[c:c + MXU, :])
    out_ref[...] = acc


def _ffn(x2, n2, wi, wo):
    tm = TOKEN_TILE
    t = x2.shape[0]
    row = lambda i: (i, 0)
    return pl.pallas_call(
        _ffn_kernel,
        name="ffn",
        grid=(t // tm,),
        in_specs=[pl.BlockSpec((tm, D_MODEL), row), _const_spec((1, D_MODEL)),
                  _const_spec((D_MODEL, 2 * FFN_HIDDEN)), _const_spec((FFN_HIDDEN, D_MODEL))],
        out_specs=pl.BlockSpec((tm, D_MODEL), row),
        out_shape=jax.ShapeDtypeStruct((t, D_MODEL), F32),
        compiler_params=_params("parallel"),
    )(x2, n2, wi, wo)


def _rope_tables(seq):
    pos = jnp.arange(seq, dtype=F32)
    inv_freq = jnp.power(jnp.float32(ROPE_THETA), -jnp.arange(0, ROPE_DIM, 2, dtype=F32) / ROPE_DIM)
    ang = pos[:, None] * inv_freq[None, :]
    cos, sin = jnp.cos(ang), jnp.sin(ang)
    half = ROPE_DIM // 2
    one = jnp.ones((seq, HEAD_DIM - ROPE_DIM), F32)
    zero = jnp.zeros((seq, HEAD_DIM - half), F32)
    cos_h = jnp.concatenate([cos, cos, one], axis=1)
    sa_h = jnp.concatenate([-sin, zero], axis=1)
    sb_h = jnp.concatenate([jnp.zeros((seq, half), F32), sin, zero[:, half:]], axis=1)
    tile = lambda t: jnp.concatenate([t, t], axis=1)
    return tile(cos_h), tile(sa_h), tile(sb_h)


def _pack_layer(l, norm1, w_in, qk_gain, sink, conv_w, a_log, dt_bias, o_gain, w_gate, w_br_a, w_br_b, w_br_c,
                w_out, norm2, w_ffn_in, w_ffn_out):
    wi = w_in[l]
    kb0, kb1 = wi[:, 5120:5184], wi[:, 5184:5248]
    vb0, vb1 = wi[:, 5248:5312], wi[:, 5312:5376]
    w1 = jnp.concatenate([
        wi[:, :5120], kb0, kb0, kb1, kb1, vb0, vb0, vb1, vb1, wi[:, 5376:7424],
        wi[:, 7424:7456], jnp.zeros((D_MODEL, K1_BA - 32), F32)], axis=1).astype(BF16)
    scale = HEAD_DIM ** -0.5 * LOG2_E
    gains = jnp.stack([qk_gain[l, 0] * scale, qk_gain[l, 1], qk_gain[l, 2] * scale, qk_gain[l, 3]])
    gains = jnp.concatenate([gains, gains], axis=1)
    pad = lambda v: jnp.concatenate([jnp.zeros((16,), F32), v.reshape(-1), jnp.zeros((LANES - 32,), F32)])[None, :]
    return dict(
        n1=norm1[l][None, :], w1=w1, gains=gains, sink=sink[l] * LOG2_E, conv_w=conv_w[l],
        alog=pad(a_log[l]), dtb=pad(dt_bias[l]), ogain=jnp.tile(o_gain[l], C_HEADS)[None, :],
        wg=w_gate[l].astype(BF16), wba=w_br_a[l].astype(BF16), wbb=w_br_b[l].astype(BF16),
        wbc=w_br_c[l].astype(BF16), wo=w_out[l].astype(BF16), n2=norm2[l][None, :],
        wfi=w_ffn_in[l].astype(BF16), wfo=w_ffn_out[l].astype(BF16))


def _layer(x2, p, tabs, batch, seq):
    t = batch * seq
    a0, a1, a2, qkv_b, xc, z, ba = _inproj(x2, p["n1"], p["w1"], *tabs, p["gains"], batch, seq)
    a_res = _mixer_a((a0.reshape(batch, 1, seq, A_QKV), a1, a2))
    o_b = _mixer_b(qkv_b, p["sink"], batch, seq)
    factors = _delta_chunks(xc.reshape(batch, seq, K1_C), ba.reshape(batch, seq, K1_BA), p["conv_w"], p["alog"],
                            p["dtb"])
    o_cf, o_cb = _delta_scan(factors)
    x2 = _merge(x2, p["n1"], a_res, o_b, o_cf.reshape(t, C_WIDTH), o_cb.reshape(t, C_WIDTH), z, p["ogain"],
                p["wg"], p["wba"], p["wbb"], p["wbc"], p["wo"], seq)
    return _ffn(x2, p["n2"], p["wfi"], p["wfo"])


def kernel(x_prompt, x_sample, norm1, w_in, qk_gain, sink, conv_w, a_log, dt_bias, o_gain, w_gate, w_br_a, w_br_b,
           w_br_c, w_out, norm2, w_ffn_in, w_ffn_out):
    depth = w_in.shape[0]
    layers = [_pack_layer(l, norm1, w_in, qk_gain, sink, conv_w, a_log, dt_bias, o_gain, w_gate, w_br_a, w_br_b,
                          w_br_c, w_out, norm2, w_ffn_in, w_ffn_out) for l in range(depth)]
    outs = []
    for x in (x_prompt, x_sample):
        batch, seq, _ = x.shape
        assert seq % (A_DILATIONS[-1] * 256) == 0
        tabs = _rope_tables(seq)
        x2 = x.reshape(batch * seq, D_MODEL)
        for p in layers:
            x2 = _layer(x2, p, tabs, batch, seq)
        outs.append(x2.reshape(batch, seq, D_MODEL))
    return tuple(outs)
```

```python
import functools

import jax
import jax.numpy as jnp
from jax import lax
from jax.experimental import pallas as pl
from jax.experimental.pallas import tpu as pltpu

F32 = jnp.float32
BF16 = jnp.bfloat16

D_MODEL = 1024
HEAD_DIM = 64
ROPE_DIM = HEAD_DIM // 4
ROPE_THETA = 500000.0
EPS = 1e-6
NEG_INF = -1e30
LOG2_E = 1.4426950408889634

A_PATTERNS = ((128, 1), (512, 4), (2048, 16))
A_DILATIONS = tuple(d for _, d in A_PATTERNS)
A_GROUPS = 3
A_WIDTH = 512
A_QKV = A_GROUPS * A_WIDTH
A_HALO = 64
B_HALF_WINDOW = 128
C_HEADS = 8
C_CHUNK = 64
C_CONV = 5
C_WIDTH = 512
FFN_HIDDEN = 2816
N_BRANCH = 3

LANES = 128
MXU = 256
VMEM_LIMIT = 56 * 1024 * 1024
TOKEN_TILE = 512

K1_A = 3 * A_QKV
K1_B = 512 + 256 + 256
K1_C = 3 * C_WIDTH
K1_Z = C_WIDTH
K1_BA = LANES
K1_OFF_B = K1_A
K1_OFF_C = K1_OFF_B + K1_B
K1_OFF_Z = K1_OFF_C + K1_C
K1_OFF_BA = K1_OFF_Z + K1_Z
K1_WIDTH = K1_OFF_BA + K1_BA


def _dot(a, b):
    return jnp.dot(a, b, preferred_element_type=F32)


def _dot_nt(a, b):
    return lax.dot_general(a, b, (((1,), (1,)), ((), ())), preferred_element_type=F32)


def _dot_tn(a, b):
    return lax.dot_general(a, b, (((0,), (0,)), ((), ())), preferred_element_type=F32)


def _split3(a):
    a1 = a.astype(BF16)
    r1 = a - a1.astype(F32)
    a2 = r1.astype(BF16)
    a3 = (r1 - a2.astype(F32)).astype(BF16)
    return a1, a2, a3


def _dot_data_sel(a, sel):
    a1, a2, a3 = _split3(a)
    return _dot(a1, sel) + _dot(a2, sel) + _dot(a3, sel)


def _dot_sel_data(sel, b):
    b1, b2, b3 = _split3(b)
    return _dot(sel, b1) + _dot(sel, b2) + _dot(sel, b3)


def _rms_rows(x, gain):
    ms = jnp.mean(x * x, axis=-1, keepdims=True)
    return x * lax.rsqrt(ms + EPS) * gain


def _sigmoid(x):
    return 0.5 * jnp.tanh(0.5 * x) + 0.5


def _head_ones(n):
    r = lax.broadcasted_iota(jnp.int32, (n, n), 0) // HEAD_DIM
    c = lax.broadcasted_iota(jnp.int32, (n, n), 1) // HEAD_DIM
    return jnp.where(r == c, 1.0, 0.0).astype(BF16)


def _const_spec(shape):
    nd = len(shape)
    return pl.BlockSpec(shape, lambda *_: (0,) * nd, pipeline_mode=pl.Buffered(1))


def _params(*sem):
    return pltpu.CompilerParams(dimension_semantics=sem, vmem_limit_bytes=VMEM_LIMIT)


def _inproj_kernel(x_ref, n1_ref, w_ref, cos_ref, sa_ref, sb_ref, gain_ref,
                   a0_ref, a1_ref, a2_ref, ob_ref, oc_ref, oz_ref, oba_ref, deint_scr):
    tm = x_ref.shape[0]
    h = _rms_rows(x_ref[...], n1_ref[...]).astype(BF16)
    ones = _head_ones(MXU)
    cos, sa, sb = cos_ref[...], sa_ref[...], sb_ref[...]
    a_refs = (a0_ref, a1_ref, a2_ref)
    slot = [0]

    def qk_norm_rope(acc, gain):
        ss = _dot((acc * acc).astype(BF16), ones)
        r = lax.rsqrt(ss * (1.0 / HEAD_DIM) + EPS)
        outs = []
        for s in range(0, MXU, LANES):
            y = acc[:, s:s + LANES] * r[:, s:s + LANES] * gain
            outs.append(y * cos + pltpu.roll(y, LANES - 8, 1) * sa + pltpu.roll(y, 8, 1) * sb)
        return outs

    def emit_a(g, col, val):
        d = A_DILATIONS[g]
        if d == 1:
            a0_ref[:, col:col + LANES] = val.astype(BF16)
            return
        s = slot[0]
        slot[0] = (s + 1) % deint_scr.shape[0]
        deint_scr[s] = val
        for r in range(d):
            a_refs[g][0, r, :, col:col + LANES] = deint_scr[s, pl.ds(r, tm // d, stride=d), :].astype(BF16)

    items = []

    def a_item(kind, g, c):
        def finish(acc):
            parts = qk_norm_rope(acc, gain_ref[kind:kind + 1, :]) if kind < 2 else [acc[:, :LANES], acc[:, LANES:]]
            for i, part in enumerate(parts):
                emit_a(g, kind * A_WIDTH + c + i * LANES, part)
        items.append((kind * A_QKV + g * A_WIDTH + c, MXU, finish))

    def qk_item(c0, gain_row, out_ref, o0):
        def finish(acc):
            for i, part in enumerate(qk_norm_rope(acc, gain_ref[gain_row:gain_row + 1, :])):
                out_ref[:, o0 + i * LANES:o0 + (i + 1) * LANES] = part.astype(out_ref.dtype)
        items.append((c0, MXU, finish))

    def plain_item(c0, width, out_ref, o0):
        def finish(acc):
            out_ref[:, o0:o0 + width] = acc.astype(out_ref.dtype)
        items.append((c0, width, finish))

    for kind in range(3):
        for g in range(A_GROUPS):
            for c in range(0, A_WIDTH, MXU):
                a_item(kind, g, c)
    for c in range(0, 512, MXU):
        qk_item(K1_OFF_B + c, 2, ob_ref, c)
    qk_item(K1_OFF_B + 512, 3, ob_ref, 512)
    plain_item(K1_OFF_B + 768, 256, ob_ref, 768)
    for c in range(0, K1_C, MXU):
        plain_item(K1_OFF_C + c, MXU, oc_ref, c)
    for c in range(0, K1_Z, MXU):
        plain_item(K1_OFF_Z + c, MXU, oz_ref, c)
    plain_item(K1_OFF_BA, K1_BA, oba_ref, 0)

    product = lambda it: _dot(h, w_ref[:, it[0]:it[0] + it[1]])
    acc = product(items[0])
    for i, it in enumerate(items):
        nxt = product(items[i + 1]) if i + 1 < len(items) else None
        it[2](acc)
        acc = nxt


def _inproj(x2, n1, w, cos_t, sa_t, sb_t, gains, batch, seq):
    tm = TOKEN_TILE
    t = x2.shape[0]
    ns = seq // tm
    row = lambda i: (i, 0)
    pos = lambda i: (i % ns, 0)
    d1, d2 = A_DILATIONS[1], A_DILATIONS[2]
    res = lambda i: (i // ns, 0, i % ns, 0)
    return pl.pallas_call(
        _inproj_kernel,
        name="inproj",
        grid=(t // tm,),
        in_specs=[
            pl.BlockSpec((tm, D_MODEL), row),
            _const_spec((1, D_MODEL)),
            _const_spec((D_MODEL, K1_WIDTH)),
            pl.BlockSpec((tm, LANES), pos),
            pl.BlockSpec((tm, LANES), pos),
            pl.BlockSpec((tm, LANES), pos),
            _const_spec((4, LANES)),
        ],
        out_specs=[
            pl.BlockSpec((tm, A_QKV), row),
            pl.BlockSpec((1, d1, tm // d1, A_QKV), res),
            pl.BlockSpec((1, d2, tm // d2, A_QKV), res),
            pl.BlockSpec((tm, K1_B), row),
            pl.BlockSpec((tm, K1_C), row),
            pl.BlockSpec((tm, K1_Z), row),
            pl.BlockSpec((tm, K1_BA), row),
        ],
        out_shape=[
            jax.ShapeDtypeStruct((t, A_QKV), BF16),
            jax.ShapeDtypeStruct((batch, d1, seq // d1, A_QKV), BF16),
            jax.ShapeDtypeStruct((batch, d2, seq // d2, A_QKV), BF16),
            jax.ShapeDtypeStruct((t, K1_B), BF16),
            jax.ShapeDtypeStruct((t, K1_C), BF16),
            jax.ShapeDtypeStruct((t, K1_Z), BF16),
            jax.ShapeDtypeStruct((t, K1_BA), F32),
        ],
        scratch_shapes=[pltpu.VMEM((4, tm, LANES), F32)],
        compiler_params=_params("parallel"),
    )(x2, n1, w, cos_t, sa_t, sb_t, gains)


def _attn_kernel(*refs, tq, halo, sub, length, kv_shared, has_sink, has_lse):
    refs = list(refs)
    sink_ref = refs.pop(0) if has_sink else None
    q_ref, kp_ref, kc_ref, kn_ref, vp_ref, vc_ref, vn_ref, o_ref = refs[:8]
    rest = refs[8:]
    lse_ref = rest.pop(0) if has_lse else None
    kbuf, vbuf = rest
    n = pl.program_id(2)
    nk = sub + 2 * halo

    kbuf[0:halo, :] = kp_ref[0, 0]
    kbuf[halo:halo + tq, :] = kc_ref[0, 0]
    kbuf[halo + tq:, :] = kn_ref[0, 0]
    vbuf[0:halo, :] = vp_ref[0, 0]
    vbuf[halo:halo + tq, :] = vc_ref[0, 0]
    vbuf[halo + tq:, :] = vn_ref[0, 0]

    rows = lax.broadcasted_iota(jnp.int32, (2 * sub, nk), 0) % sub
    cols = lax.broadcasted_iota(jnp.int32, (2 * sub, nk), 1)
    rel = cols - halo - rows
    band = (rel <= halo) & (rel >= -halo)
    lane_low = lax.broadcasted_iota(jnp.int32, (2 * sub, LANES), 1) < HEAD_DIM
    row_first = lax.broadcasted_iota(jnp.int32, (2 * sub, LANES), 0) < sub
    own_lanes = lane_low == row_first
    low_lane = lax.broadcasted_iota(jnp.int32, (sub, LANES), 1) < HEAD_DIM
    first_row = lax.broadcasted_iota(jnp.int32, (2 * sub, 1), 0) < sub

    for s in range(tq // sub):
        kpos = n * tq + (s * sub - halo) + cols
        mask = band & (kpos >= 0) & (kpos < length)
        qrows = slice(s * sub, (s + 1) * sub)
        krows = slice(s * sub, s * sub + nk)
        kcol = [(hp // 2 if kv_shared else hp) * LANES for hp in range(4)]
        sc = []
        for hp in range(4):
            qp = q_ref[0, 0, qrows, hp * LANES:(hp + 1) * LANES]
            qm = jnp.where(own_lanes, jnp.concatenate([qp, qp], axis=0), jnp.zeros((), qp.dtype))
            sc.append(jnp.where(mask, _dot_nt(qm, kbuf[krows, kcol[hp]:kcol[hp] + LANES]), NEG_INF))
        m = [jnp.max(x, axis=-1, keepdims=True) for x in sc]
        if has_sink:
            snk = [jnp.where(first_row, sink_ref[2 * hp], sink_ref[2 * hp + 1]) for hp in range(4)]
            m = [jnp.maximum(x, y) for x, y in zip(m, snk)]
        p = [jnp.exp2(x - y) for x, y in zip(sc, m)]
        den = [jnp.sum(x, axis=-1, keepdims=True) for x in p]
        if has_sink:
            den = [x + jnp.exp2(y - z) for x, y, z in zip(den, snk, m)]
        o = [_dot(x.astype(BF16), vbuf[krows, kc0:kc0 + LANES]) / y for x, kc0, y in zip(p, kcol, den)]
        for hp in range(4):
            lanes = slice(hp * LANES, (hp + 1) * LANES)
            o_ref[0, 0, qrows, lanes] = jnp.where(low_lane, o[hp][:sub], o[hp][sub:]).astype(o_ref.dtype)
            if has_lse:
                lse = jnp.broadcast_to(m[hp] + jnp.log2(den[hp]), (2 * sub, LANES))
                lse_ref[0, 0, qrows, lanes] = jnp.where(low_lane, lse[:sub], lse[sub:])


def _banded_attention(qkv, *, name, q_cb, k_cb, v_cb, kv_width, halo, tq, sub, kv_shared, sink=None,
                      has_lse=False, out_dtype=F32):
    batch, nres, length, _ = qkv.shape
    nb = length // tq
    per = tq // halo
    last = length // halo - 1
    q_spec = pl.BlockSpec((1, 1, tq, 512), lambda b, r, n: (b, r, n, q_cb))

    def kv_specs(cb):
        return [
            pl.BlockSpec((1, 1, halo, kv_width), lambda b, r, n: (b, r, jnp.maximum(n * per - 1, 0), cb)),
            pl.BlockSpec((1, 1, tq, kv_width), lambda b, r, n: (b, r, n, cb)),
            pl.BlockSpec((1, 1, halo, kv_width), lambda b, r, n: (b, r, jnp.minimum((n + 1) * per, last), cb)),
        ]

    in_specs = [q_spec] + kv_specs(k_cb) + kv_specs(v_cb)
    args = [qkv] * 7
    if sink is not None:
        in_specs = [pl.BlockSpec(memory_space=pltpu.SMEM)] + in_specs
        args = [sink] + args
    o_spec = pl.BlockSpec((1, 1, tq, 512), lambda b, r, n: (b, r, n, 0))
    out_specs = [o_spec]
    out_shape = [jax.ShapeDtypeStruct((batch, nres, length, 512), out_dtype)]
    if has_lse:
        out_specs.append(o_spec)
        out_shape.append(jax.ShapeDtypeStruct((batch, nres, length, 512), F32))
    kern = functools.partial(_attn_kernel, tq=tq, halo=halo, sub=sub, length=length, kv_shared=kv_shared,
                             has_sink=sink is not None, has_lse=has_lse)
    return pl.pallas_call(
        kern,
        name=name,
        grid=(batch, nres, nb),
        in_specs=in_specs,
        out_specs=out_specs,
        out_shape=out_shape,
        scratch_shapes=[pltpu.VMEM((tq + 2 * halo, kv_width), BF16), pltpu.VMEM((tq + 2 * halo, kv_width), BF16)],
        compiler_params=_params("parallel", "parallel", "parallel"),
    )(*args)


def _mixer_a(groups):
    res = []
    for g, qkv in enumerate(groups):
        res.append(_banded_attention(qkv, name=f"attn_a{g}", q_cb=0, k_cb=1, v_cb=2, kv_width=512, halo=A_HALO,
                                     tq=min(512, qkv.shape[2]), sub=128, kv_shared=False, has_lse=True))
    return res


def _mixer_b(qkv_b, sink, batch, seq):
    (o,) = _banded_attention(
        qkv_b.reshape(batch, 1, seq, K1_B), name="attn_b", q_cb=0, k_cb=2, v_cb=3, kv_width=256,
        halo=B_HALF_WINDOW, tq=512, sub=128, kv_shared=True, sink=sink, out_dtype=BF16)
    return o.reshape(batch * seq, 512)


C_TILE = 512
SCAN_TILE = 256
C_HALO = 16
GROUP = 256
N_CHUNK = C_TILE // C_CHUNK
SCAN_CHUNKS = SCAN_TILE // C_CHUNK


def _head_mask():
    r = lax.broadcasted_iota(jnp.int32, (GROUP, GROUP), 0) // HEAD_DIM
    c = lax.broadcasted_iota(jnp.int32, (GROUP, GROUP), 1) // HEAD_DIM
    return r == c


def _blockdiag(x, headmask):
    xb = x.astype(BF16)
    return jnp.concatenate([xb, xb, xb, xb], axis=0) * headmask


def _delta_chunks_kernel(xp_ref, xc_ref, xn_ref, ba_ref, convw_ref, alog_ref, dtb_ref,
                         uf_ref, wf_ref, qkf_ref, qgf_ref, kdf_ref, glf_ref,
                         ub_ref, wb_ref, qkb_ref, qgb_ref, kdb_ref, glb_ref,
                         ext_scr, q_scr, k_scr, v_scr, beta_scr, gi_scr):
    n = pl.program_id(1)
    nt = pl.num_programs(1)
    outs = ((uf_ref, wf_ref, qkf_ref, qgf_ref, kdf_ref, glf_ref), (ub_ref, wb_ref, qkb_ref, qgb_ref, kdb_ref, glb_ref))

    ext_scr[0:C_HALO, :] = jnp.where(n > 0, xp_ref[0].astype(F32), 0.0)
    ext_scr[C_HALO:C_HALO + C_TILE, :] = xc_ref[0].astype(F32)
    ext_scr[C_HALO + C_TILE:, :] = jnp.where(n < nt - 1, xn_ref[0].astype(F32), 0.0)
    acc = None
    for tap in range(C_CONV):
        off = C_HALO - C_CONV // 2 + tap
        term = ext_scr[off:off + C_TILE, :] * convw_ref[tap:tap + 1, :]
        acc = term if acc is None else acc + term
    y = acc * _sigmoid(acc)
    ones = _head_ones(GROUP)
    for gi in range(2):
        sl = slice(gi * GROUP, (gi + 1) * GROUP)
        q = y[:, sl]
        k = y[:, 512 + gi * GROUP:512 + (gi + 1) * GROUP]
        ssq = _dot((q * q).astype(BF16), ones)
        ssk = _dot((k * k).astype(BF16), ones)
        q_scr[:, sl] = q * lax.rsqrt(ssq + EPS) * (HEAD_DIM ** -0.5)
        k_scr[:, sl] = k * lax.rsqrt(ssk + EPS)
    v_scr[...] = y[:, 1024:]

    ba = ba_ref[0]
    beta_all = _sigmoid(ba)
    zz = ba + dtb_ref[...]
    g_all = -jnp.exp(alog_ref[...]) * (jnp.maximum(zz, 0.0) + jnp.log(1.0 + jnp.exp(-jnp.abs(zz))))
    lane_head = lax.broadcasted_iota(jnp.int32, (LANES, 512), 1) // HEAD_DIM
    krow = lax.broadcasted_iota(jnp.int32, (LANES, 512), 0)
    ti = lax.broadcasted_iota(jnp.int32, (MXU, MXU), 0)
    tj = lax.broadcasted_iota(jnp.int32, (MXU, MXU), 1)
    same_chunk = (ti // C_CHUNK) == (tj // C_CHUNK)
    for d in range(2):
        e_beta = jnp.where(krow == lane_head + d * C_HEADS, 1.0, 0.0).astype(BF16)
        e_g = jnp.where(krow == lane_head + (2 + d) * C_HEADS, 1.0, 0.0).astype(BF16)
        beta_scr[d] = _dot_data_sel(beta_all, e_beta)
        tri = jnp.where(same_chunk & ((tj <= ti) if d == 0 else (tj >= ti)), 1.0, 0.0).astype(BF16)
        for r in range(0, C_TILE, MXU):
            gi_scr[d, r:r + MXU, :] = _dot_data_sel(_dot_sel_data(tri, g_all[r:r + MXU]), e_g)

    ii = lax.broadcasted_iota(jnp.int32, (C_CHUNK, 512), 0)
    jj = lax.broadcasted_iota(jnp.int32, (C_CHUNK, 512), 1) % C_CHUNK
    ig = lax.broadcasted_iota(jnp.int32, (C_CHUNK, GROUP), 0)
    jg = lax.broadcasted_iota(jnp.int32, (C_CHUNK, GROUP), 1) % C_CHUNK
    incl = (ii >= jj, ii <= jj)
    incl_g = (ig >= jg, ig <= jg)
    strict_g = (ig > jg, ig < jg)
    eye = jnp.where(ig == jg, 1.0, 0.0)
    diag16 = (ig // 16) == (jg // 16)
    headmask = ones
    zero_rows = jnp.zeros((LANES - C_CHUNK, LANES), F32)

    def bd(qs):
        return [_blockdiag(q, headmask) for q in qs]

    def mm(ps, bds):
        return [_dot(p.astype(BF16), b) for p, b in zip(ps, bds)]

    def mm2(ps, rs, bds):
        both = [_dot(jnp.concatenate([p, r], axis=0).astype(BF16), b) for p, r, b in zip(ps, rs, bds)]
        return [x[:C_CHUNK] for x in both], [x[C_CHUNK:] for x in both]

    def add(xs, ys):
        return [x + y for x, y in zip(xs, ys)]

    def unit_inverses(ls):
        dg = [jnp.where(diag16, l, 0.0) for l in ls]
        og = [l - d for l, d in zip(ls, dg)]
        di = [eye - d for d in dg]
        d2 = mm(dg, bd(dg))
        d4, t = mm2(d2, di, bd(d2))
        di = add(di, t)
        d8, t = mm2(d4, di, bd(d4))
        di = add(di, t)
        di = add(di, mm(di, bd(d8)))
        nn = mm(di, bd(og))
        m = [eye - x for x in nn]
        nbd = bd(nn)
        m = add(m, mm(mm(m, nbd), nbd))
        return mm(m, bd(di))

    def transposed_decay(gi_):
        parts = []
        for p in range(512 // LANES):
            t = jnp.concatenate([gi_[:, p * LANES:(p + 1) * LANES], zero_rows], axis=0).T
            parts.append(t[:C_CHUNK, :] + pltpu.roll(t[C_CHUNK:, :], HEAD_DIM, 1))
        return jnp.concatenate(parts, axis=1)

    def chunks(cs):
        lms, vbs, kbgs, dest = [], [], [], []
        for c in cs:
            rows = pl.ds(pl.multiple_of(c * C_CHUNK, C_CHUNK), C_CHUNK)
            qc, kc, vc = q_scr[rows, :], k_scr[rows, :], v_scr[rows, :]
            kk, qk = [], []
            for gidx in range(2):
                sl = slice(gidx * GROUP, (gidx + 1) * GROUP)
                kbd = _blockdiag(kc[:, sl], headmask)
                kk.append(_dot_nt(kc[:, sl].astype(BF16), kbd))
                qk.append(_dot_nt(qc[:, sl].astype(BF16), kbd))
            for d in range(2):
                u_ref, w_ref, qk_ref, qg_ref, kd_ref, gl_ref = outs[d]
                beta = beta_scr[d, rows, :]
                gi_ = gi_scr[d, rows, :]
                decay = jnp.exp(jnp.where(incl[d], gi_ - transposed_decay(gi_), NEG_INF))
                eg = jnp.exp(gi_)
                last = C_CHUNK - 1 if d == 0 else 0
                glrow = gi_[last:last + 1, :]
                kd_ref[0, rows, :] = (kc * jnp.exp(glrow - gi_)).astype(BF16)
                qg_ref[0, rows, :] = (qc * eg).astype(BF16)
                gl_ref[0, pl.ds(c, 1), 0, :] = jnp.exp(glrow)
                kbg = kc * beta * eg
                vb = vc * beta
                for gidx in range(2):
                    sl = slice(gidx * GROUP, (gidx + 1) * GROUP)
                    lms.append(jnp.where(strict_g[d], kk[gidx] * beta[:, sl] * decay[:, sl], 0.0))
                    qk_ref[0, rows, sl] = jnp.where(incl_g[d], qk[gidx] * decay[:, sl], 0.0).astype(BF16)
                    vbs.append(vb[:, sl])
                    kbgs.append(kbg[:, sl])
                    dest.append((u_ref, w_ref, rows, sl))
        tinvs = unit_inverses(lms)
        uw = mm(tinvs, [jnp.concatenate([a, b], axis=1) for a, b in zip(bd(vbs), bd(kbgs))])
        for (u_ref, w_ref, rows, sl), x in zip(dest, uw):
            u_ref[0, rows, sl] = x[:, :GROUP].astype(BF16)
            w_ref[0, rows, sl] = x[:, GROUP:].astype(BF16)

    def body(c2, carry):
        chunks((2 * c2, 2 * c2 + 1))
        return carry

    lax.fori_loop(0, N_CHUNK // 2, body, 0)


def _delta_chunks(xc, ba, conv_w, alog_row, dtb_row):
    batch, seq, _ = xc.shape
    nt = seq // C_TILE
    per = C_TILE // C_HALO
    last = seq // C_HALO - 1
    tile = lambda b, n: (b, n, 0)
    wide = lambda dt: jax.ShapeDtypeStruct((batch, seq, C_WIDTH), dt)
    gl_sds = jax.ShapeDtypeStruct((batch, seq // C_CHUNK, 1, C_WIDTH), F32)
    out_shape = [wide(BF16), wide(BF16), wide(BF16), wide(BF16), wide(BF16), gl_sds] * 2
    wide_spec = pl.BlockSpec((1, C_TILE, C_WIDTH), tile)
    gl_spec = pl.BlockSpec((1, N_CHUNK, 1, C_WIDTH), lambda b, n: (b, n, 0, 0))
    return pl.pallas_call(
        _delta_chunks_kernel,
        name="delta_chunks",
        grid=(batch, nt),
        in_specs=[
            pl.BlockSpec((1, C_HALO, K1_C), lambda b, n: (b, jnp.maximum(n * per - 1, 0), 0)),
            pl.BlockSpec((1, C_TILE, K1_C), tile),
            pl.BlockSpec((1, C_HALO, K1_C), lambda b, n: (b, jnp.minimum((n + 1) * per, last), 0)),
            pl.BlockSpec((1, C_TILE, K1_BA), tile),
            _const_spec((C_CONV, K1_C)), _const_spec((1, LANES)), _const_spec((1, LANES))],
        out_specs=[wide_spec] * 5 + [gl_spec] + [wide_spec] * 5 + [gl_spec],
        out_shape=out_shape,
        scratch_shapes=[
            pltpu.VMEM((C_TILE + 2 * C_HALO, K1_C), F32),
            pltpu.VMEM((C_TILE, C_WIDTH), F32),
            pltpu.VMEM((C_TILE, C_WIDTH), F32),
            pltpu.VMEM((C_TILE, C_WIDTH), F32),
            pltpu.VMEM((2, C_TILE, C_WIDTH), F32),
            pltpu.VMEM((2, C_TILE, C_WIDTH), F32),
        ],
        compiler_params=_params("parallel", "parallel"),
    )(xc, xc, xc, ba, conv_w, alog_row, dtb_row)


def _delta_scan_kernel(uf_ref, wf_ref, qkf_ref, qgf_ref, kdf_ref, glf_ref,
                       ub_ref, wb_ref, qkb_ref, qgb_ref, kdb_ref, glb_ref, of_ref, ob_ref, state_scr):
    n = pl.program_id(1)
    bblk = uf_ref.shape[0]
    ins = ((uf_ref, wf_ref, qkf_ref, qgf_ref, kdf_ref, glf_ref, of_ref),
           (ub_ref, wb_ref, qkb_ref, qgb_ref, kdb_ref, glb_ref, ob_ref))

    @pl.when(n == 0)
    def _():
        state_scr[...] = jnp.zeros_like(state_scr)

    headmask = _head_mask()
    head01 = _head_ones(GROUP)

    def body(c, carry):
        chains = []
        for d in range(2):
            cc = c if d == 0 else SCAN_CHUNKS - 1 - c
            rows = pl.ds(pl.multiple_of(cc * C_CHUNK, C_CHUNK), C_CHUNK)
            for b in range(bblk):
                for gidx in range(2):
                    chains.append((d, b, gidx, cc, rows, slice(gidx * GROUP, (gidx + 1) * GROUP)))
        states = [state_scr[b, d, gidx] for d, b, gidx, _, _, _ in chains]
        both = [_dot(jnp.concatenate([ins[d][1][b, rows, sl], ins[d][3][b, rows, sl]], axis=0), st.astype(BF16))
                for (d, b, _, _, rows, sl), st in zip(chains, states)]
        v_new = [ins[d][0][b, rows, sl].astype(F32) - bo[:C_CHUNK] for (d, b, _, _, rows, sl), bo in zip(chains, both)]
        intra = [_dot(ins[d][2][b, rows, sl], _blockdiag(v, head01)) for (d, b, _, _, rows, sl), v in zip(chains, v_new)]
        upd = [_dot_tn(ins[d][4][b, rows, sl], v.astype(BF16)) for (d, b, _, _, rows, sl), v in zip(chains, v_new)]
        for (d, b, gidx, cc, rows, sl), st, bo, it, up in zip(chains, states, both, intra, upd):
            ins[d][6][b, rows, sl] = bo[C_CHUNK:] + it
            gl = ins[d][5][b, pl.ds(cc, 1), 0, :]
            state_scr[b, d, gidx] = st * gl[:, sl] + jnp.where(headmask, up, 0.0)
        return carry

    lax.fori_loop(0, SCAN_CHUNKS, body, 0)


def _delta_scan(factors):
    batch, seq, _ = factors[0].shape
    nt = seq // SCAN_TILE
    bblk = min(batch, 4)
    fwd = lambda b, n: (b, n, 0)
    bwd = lambda b, n: (b, nt - 1 - n, 0)

    def specs(tile):
        wide = pl.BlockSpec((bblk, SCAN_TILE, C_WIDTH), tile)
        gl = pl.BlockSpec((bblk, SCAN_CHUNKS, 1, C_WIDTH), lambda b, n: tile(b, n) + (0,))
        return [wide] * 5 + [gl]

    out_sds = jax.ShapeDtypeStruct((batch, seq, C_WIDTH), F32)
    return pl.pallas_call(
        _delta_scan_kernel,
        name="delta_scan",
        grid=(batch // bblk, nt),
        in_specs=specs(fwd) + specs(bwd),
        out_specs=[pl.BlockSpec((bblk, SCAN_TILE, C_WIDTH), fwd), pl.BlockSpec((bblk, SCAN_TILE, C_WIDTH), bwd)],
        out_shape=[out_sds, out_sds],
        scratch_shapes=[pltpu.VMEM((bblk, 2, 2, GROUP, GROUP), F32)],
        compiler_params=_params("parallel", "arbitrary"),
    )(*factors)


def _merge_kernel(x_ref, n1_ref, oa0, la0, oa1, la1, oa2, la2, ob_ref, ocf_ref, ocb_ref, z_ref, ogain_ref,
                  wg_ref, wba_ref, wbb_ref, wbc_ref, wo_ref, out_ref, inter_scr):
    tm = x_ref.shape[0]
    x = x_ref[...]
    h = _rms_rows(x, n1_ref[...]).astype(BF16)

    def interleaved(ref, d, base):
        for r in range(d):
            for c in range(512 // LANES):
                inter_scr[base + c, pl.ds(r, tm // d, stride=d), :] = ref[0, r, :, c * LANES:(c + 1) * LANES]
        return jnp.concatenate([inter_scr[base + c] for c in range(512 // LANES)], axis=1)

    def branch(br, o_br, w_ref):
        gate = _sigmoid(_dot(h, wg_ref[:, br * D_MODEL:(br + 1) * D_MODEL]))
        return gate * _dot(o_br.astype(BF16), w_ref[...])

    term_b = branch(1, ob_ref[...], wbb_ref)

    d1, d2 = A_DILATIONS[1], A_DILATIONS[2]
    l0, o0 = la0[...], oa0[...]
    l1, o1 = interleaved(la1, d1, 0), interleaved(oa1, d1, 4)
    l2, o2 = interleaved(la2, d2, 8), interleaved(oa2, d2, 12)
    m = jnp.maximum(jnp.maximum(l0, l1), l2)
    e0, e1, e2 = jnp.exp2(l0 - m), jnp.exp2(l1 - m), jnp.exp2(l2 - m)
    o_a = (e0 * o0 + e1 * o1 + e2 * o2) / (e0 + e1 + e2)
    merged = branch(0, o_a, wba_ref) + term_b
    oc = ocf_ref[...] + ocb_ref[...]
    ones = _head_ones(MXU)
    z = z_ref[...].astype(F32)
    parts = []
    for c in range(0, C_WIDTH, MXU):
        blk = oc[:, c:c + MXU]
        ss = _dot((blk * blk).astype(BF16), ones)
        zz = z[:, c:c + MXU]
        parts.append(blk * lax.rsqrt(ss * (1.0 / HEAD_DIM) + EPS) * ogain_ref[:, c:c + MXU] * (zz * _sigmoid(zz)))
    o_c = jnp.concatenate(parts, axis=1)
    merged = merged + branch(2, o_c, wbc_ref)
    out_ref[...] = x + _dot(merged.astype(BF16), wo_ref[...])


def _merge(x2, n1, a_res, o_b, o_cf, o_cb, z, ogain, wg, wba, wbb, wbc, wo, seq):
    tm = TOKEN_TILE
    t = x2.shape[0]
    ns = seq // tm
    row = lambda i: (i, 0)
    res = lambda i: (i // ns, 0, i % ns, 0)
    wide = pl.BlockSpec((tm, D_MODEL), row)
    half = pl.BlockSpec((tm, 512), row)
    d1, d2 = A_DILATIONS[1], A_DILATIONS[2]
    res1 = pl.BlockSpec((1, d1, tm // d1, 512), res)
    res2 = pl.BlockSpec((1, d2, tm // d2, 512), res)
    (o0, l0), (o1, l1), (o2, l2) = a_res
    return pl.pallas_call(
        _merge_kernel,
        name="merge",
        grid=(t // tm,),
        in_specs=[wide, _const_spec((1, D_MODEL)), half, half, res1, res1, res2, res2] + [half] * 4 + [
            _const_spec((1, C_WIDTH)), _const_spec((D_MODEL, N_BRANCH * D_MODEL)), _const_spec((512, D_MODEL)),
            _const_spec((512, D_MODEL)), _const_spec((512, D_MODEL)), _const_spec((D_MODEL, D_MODEL))],
        out_specs=wide,
        out_shape=jax.ShapeDtypeStruct((t, D_MODEL), F32),
        scratch_shapes=[pltpu.VMEM((16, tm, LANES), F32)],
        compiler_params=_params("parallel"),
    )(x2, n1, o0.reshape(t, 512), l0.reshape(t, 512), o1, l1, o2, l2, o_b, o_cf, o_cb, z, ogain, wg, wba, wbb, wbc, wo)


def _ffn_kernel(x_ref, n2_ref, wi_ref, wo_ref, out_ref):
    x = x_ref[...]
    h = _rms_rows(x, n2_ref[...]).astype(BF16)
    acc = x
    for c in range(0, FFN_HIDDEN, MXU):
        gate = _dot(h, wi_ref[:, c:c + MXU])
        up = _dot(h, wi_ref[:, FFN_HIDDEN + c:FFN_HIDDEN + c + MXU])
        act = (gate * _sigmoid(gate) * up).astype(BF16)
        acc = acc + _dot(act, wo_ref[c:c + MXU, :])
    out_ref[...] = acc


def _ffn(x2, n2, wi, wo):
    tm = TOKEN_TILE
    t = x2.shape[0]
    row = lambda i: (i, 0)
    return pl.pallas_call(
        _ffn_kernel,
        name="ffn",
        grid=(t // tm,),
        in_specs=[pl.BlockSpec((tm, D_MODEL), row), _const_spec((1, D_MODEL)),
                  _const_spec((D_MODEL, 2 * FFN_HIDDEN)), _const_spec((FFN_HIDDEN, D_MODEL))],
        out_specs=pl.BlockSpec((tm, D_MODEL), row),
        out_shape=jax.ShapeDtypeStruct((t, D_MODEL), F32),
        compiler_params=_params("parallel"),
    )(x2, n2, wi, wo)


def _rope_tables(seq):
    pos = jnp.arange(seq, dtype=F32)
    inv_freq = jnp.power(jnp.float32(ROPE_THETA), -jnp.arange(0, ROPE_DIM, 2, dtype=F32) / ROPE_DIM)
    ang = pos[:, None] * inv_freq[None, :]
    cos, sin = jnp.cos(ang), jnp.sin(ang)
    half = ROPE_DIM // 2
    one = jnp.ones((seq, HEAD_DIM - ROPE_DIM), F32)
    zero = jnp.zeros((seq, HEAD_DIM - half), F32)
    cos_h = jnp.concatenate([cos, cos, one], axis=1)
    sa_h = jnp.concatenate([-sin, zero], axis=1)
    sb_h = jnp.concatenate([jnp.zeros((seq, half), F32), sin, zero[:, half:]], axis=1)
    tile = lambda t: jnp.concatenate([t, t], axis=1)
    return tile(cos_h), tile(sa_h), tile(sb_h)


def _pack_layer(l, norm1, w_in, qk_gain, sink, conv_w, a_log, dt_bias, o_gain, w_gate, w_br_a, w_br_b, w_br_c,
                w_out, norm2, w_ffn_in, w_ffn_out):
    wi = w_in[l]
    kb0, kb1 = wi[:, 5120:5184], wi[:, 5184:5248]
    vb0, vb1 = wi[:, 5248:5312], wi[:, 5312:5376]
    w1 = jnp.concatenate([
        wi[:, :5120], kb0, kb0, kb1, kb1, vb0, vb0, vb1, vb1, wi[:, 5376:7424],
        wi[:, 7424:7456], jnp.zeros((D_MODEL, K1_BA - 32), F32)], axis=1).astype(BF16)
    scale = HEAD_DIM ** -0.5 * LOG2_E
    gains = jnp.stack([qk_gain[l, 0] * scale, qk_gain[l, 1], qk_gain[l, 2] * scale, qk_gain[l, 3]])
    gains = jnp.concatenate([gains, gains], axis=1)
    pad = lambda v: jnp.concatenate([jnp.zeros((16,), F32), v.reshape(-1), jnp.zeros((LANES - 32,), F32)])[None, :]
    return dict(
        n1=norm1[l][None, :], w1=w1, gains=gains, sink=sink[l] * LOG2_E, conv_w=conv_w[l],
        alog=pad(a_log[l]), dtb=pad(dt_bias[l]), ogain=jnp.tile(o_gain[l], C_HEADS)[None, :],
        wg=w_gate[l].astype(BF16), wba=w_br_a[l].astype(BF16), wbb=w_br_b[l].astype(BF16),
        wbc=w_br_c[l].astype(BF16), wo=w_out[l].astype(BF16), n2=norm2[l][None, :],
        wfi=w_ffn_in[l].astype(BF16), wfo=w_ffn_out[l].astype(BF16))


def _layer(x2, p, tabs, batch, seq):
    t = batch * seq
    a0, a1, a2, qkv_b, xc, z, ba = _inproj(x2, p["n1"], p["w1"], *tabs, p["gains"], batch, seq)
    a_res = _mixer_a((a0.reshape(batch, 1, seq, A_QKV), a1, a2))
    o_b = _mixer_b(qkv_b, p["sink"], batch, seq)
    factors = _delta_chunks(xc.reshape(batch, seq, K1_C), ba.reshape(batch, seq, K1_BA), p["conv_w"], p["alog"],
                            p["dtb"])
    o_cf, o_cb = _delta_scan(factors)
    x2 = _merge(x2, p["n1"], a_res, o_b, o_cf.reshape(t, C_WIDTH), o_cb.reshape(t, C_WIDTH), z, p["ogain"],
                p["wg"], p["wba"], p["wbb"], p["wbc"], p["wo"], seq)
    return _ffn(x2, p["n2"], p["wfi"], p["wfo"])


def kernel(x_prompt, x_sample, norm1, w_in, qk_gain, sink, conv_w, a_log, dt_bias, o_gain, w_gate, w_br_a, w_br_b,
           w_br_c, w_out, norm2, w_ffn_in, w_ffn_out):
    depth = w_in.shape[0]
    layers = [_pack_layer(l, norm1, w_in, qk_gain, sink, conv_w, a_log, dt_bias, o_gain, w_gate, w_br_a, w_br_b,
                          w_br_c, w_out, norm2, w_ffn_in, w_ffn_out) for l in range(depth)]
    outs = []
    for x in (x_prompt, x_sample):
        batch, seq, _ = x.shape
        assert seq % (A_DILATIONS[-1] * 256) == 0
        tabs = _rope_tables(seq)
        x2 = x.reshape(batch * seq, D_MODEL)
        for p in layers:
            x2 = _layer(x2, p, tabs, batch, seq)
        outs.append(x2.reshape(batch, seq, D_MODEL))
    return tuple(outs)
```

```python
import functools

import jax
import jax.numpy as jnp
from jax import lax
from jax.experimental import pallas as pl
from jax.experimental.pallas import tpu as pltpu

F32 = jnp.float32
BF16 = jnp.bfloat16

D_MODEL = 1024
HEAD_DIM = 64
ROPE_DIM = HEAD_DIM // 4
ROPE_THETA = 500000.0
EPS = 1e-6
NEG_INF = -1e30
LOG2_E = 1.4426950408889634

A_PATTERNS = ((128, 1), (512, 4), (2048, 16))
A_DILATIONS = tuple(d for _, d in A_PATTERNS)
A_GROUPS = 3
A_WIDTH = 512
A_QKV = A_GROUPS * A_WIDTH
A_HALO = 64
B_HALF_WINDOW = 128
C_HEADS = 8
C_CHUNK = 64
C_CONV = 5
C_WIDTH = 512
FFN_HIDDEN = 2816
N_BRANCH = 3

LANES = 128
MXU = 256
VMEM_LIMIT = 56 * 1024 * 1024
TOKEN_TILE = 512
IN_HALO = 16

K1_A = 3 * A_QKV
K1_B = 512 + 256 + 256
K1_C = 3 * C_WIDTH
K1_Z = C_WIDTH
K1_BA = LANES
K1_OFF_B = K1_A
K1_OFF_C = K1_OFF_B + K1_B
K1_OFF_Z = K1_OFF_C + K1_C
K1_OFF_BA = K1_OFF_Z + K1_Z
K1_WIDTH = K1_OFF_BA + K1_BA


def _dot(a, b):
    return jnp.dot(a, b, preferred_element_type=F32)


def _dot_nt(a, b):
    return lax.dot_general(a, b, (((1,), (1,)), ((), ())), preferred_element_type=F32)


def _dot_tn(a, b):
    return lax.dot_general(a, b, (((0,), (0,)), ((), ())), preferred_element_type=F32)


def _split3(a):
    a1 = a.astype(BF16)
    r1 = a - a1.astype(F32)
    a2 = r1.astype(BF16)
    a3 = (r1 - a2.astype(F32)).astype(BF16)
    return a1, a2, a3


def _dot_data_sel(a, sel):
    a1, a2, a3 = _split3(a)
    return _dot(a1, sel) + _dot(a2, sel) + _dot(a3, sel)


def _dot_unit_sel(a, sel):
    a1, a2, _ = _split3(a)
    return _dot(a1, sel) + _dot(a2, sel)


def _dot_sel_data(sel, b):
    b1, b2, b3 = _split3(b)
    return _dot(sel, b1) + _dot(sel, b2) + _dot(sel, b3)


def _rms_rows(x, gain):
    ms = jnp.mean(x * x, axis=-1, keepdims=True)
    return x * lax.rsqrt(ms + EPS) * gain


def _sigmoid(x):
    return 0.5 * jnp.tanh(0.5 * x) + 0.5


def _head_ones(n):
    r = lax.broadcasted_iota(jnp.int32, (n, n), 0) // HEAD_DIM
    c = lax.broadcasted_iota(jnp.int32, (n, n), 1) // HEAD_DIM
    return jnp.where(r == c, 1.0, 0.0).astype(BF16)


def _const_spec(shape):
    nd = len(shape)
    return pl.BlockSpec(shape, lambda *_: (0,) * nd, pipeline_mode=pl.Buffered(1))


def _params(*sem):
    return pltpu.CompilerParams(dimension_semantics=sem, vmem_limit_bytes=VMEM_LIMIT)


def _inproj_kernel(x_ref, xp_ref, xn_ref, n1_ref, w_ref, cos_ref, sa_ref, sb_ref, gain_ref, convw_ref,
                   a0_ref, a1_ref, a2_ref, ob_ref, oc_ref, oz_ref, oba_ref, deint_scr, conv_scr, *, tiles_per_seq):
    tm = x_ref.shape[0]
    x_ext = jnp.concatenate([xp_ref[...], x_ref[...], xn_ref[...]], axis=0)
    h_ext = _rms_rows(x_ext, n1_ref[...]).astype(BF16)
    h = h_ext[IN_HALO:IN_HALO + tm]
    ones = _head_ones(MXU)
    tile_in_seq = pl.program_id(0) % tiles_per_seq
    ext_row = lax.broadcasted_iota(jnp.int32, (tm + 2 * IN_HALO, 1), 0)
    in_seq = (((ext_row >= IN_HALO) | (tile_in_seq > 0))
              & ((ext_row < IN_HALO + tm) | (tile_in_seq < tiles_per_seq - 1)))
    cos, sa, sb = cos_ref[...], sa_ref[...], sb_ref[...]
    a_refs = (a0_ref, a1_ref, a2_ref)
    slot = [0]

    def qk_norm_rope(acc, gain):
        ss = _dot((acc * acc).astype(BF16), ones)
        r = lax.rsqrt(ss * (1.0 / HEAD_DIM) + EPS)
        outs = []
        for s in range(0, MXU, LANES):
            y = acc[:, s:s + LANES] * r[:, s:s + LANES] * gain
            outs.append(y * cos + pltpu.roll(y, LANES - 8, 1) * sa + pltpu.roll(y, 8, 1) * sb)
        return outs

    def emit_a(g, col, val):
        d = A_DILATIONS[g]
        if d == 1:
            a0_ref[:, col:col + LANES] = val.astype(BF16)
            return
        s = slot[0]
        slot[0] = (s + 1) % deint_scr.shape[0]
        deint_scr[s] = val
        for r in range(d):
            a_refs[g][0, r, :, col:col + LANES] = deint_scr[s, pl.ds(r, tm // d, stride=d), :].astype(BF16)

    items = []

    def a_item(kind, g, c):
        def finish(acc):
            parts = qk_norm_rope(acc, gain_ref[kind:kind + 1, :]) if kind < 2 else [acc[:, :LANES], acc[:, LANES:]]
            for i, part in enumerate(parts):
                emit_a(g, kind * A_WIDTH + c + i * LANES, part)
        items.append((kind * A_QKV + g * A_WIDTH + c, MXU, finish))

    def qk_item(c0, gain_row, out_ref, o0):
        def finish(acc):
            for i, part in enumerate(qk_norm_rope(acc, gain_ref[gain_row:gain_row + 1, :])):
                out_ref[:, o0 + i * LANES:o0 + (i + 1) * LANES] = part.astype(out_ref.dtype)
        items.append((c0, MXU, finish))

    def plain_item(c0, width, out_ref, o0):
        def finish(acc):
            out_ref[:, o0:o0 + width] = acc.astype(out_ref.dtype)
        items.append((c0, width, finish))

    for kind in range(3):
        for g in range(A_GROUPS):
            for c in range(0, A_WIDTH, MXU):
                a_item(kind, g, c)
    for c in range(0, 512, MXU):
        qk_item(K1_OFF_B + c, 2, ob_ref, c)
    qk_item(K1_OFF_B + 512, 3, ob_ref, 512)
    plain_item(K1_OFF_B + 768, 256, ob_ref, 768)
    def conv_item(j):
        c = j * MXU

        def finish(acc):
            buf = j % conv_scr.shape[0]
            conv_scr[buf] = jnp.where(in_seq, acc, 0.0)
            y = None
            for tap in range(C_CONV):
                off = IN_HALO - C_CONV // 2 + tap
                term = conv_scr[buf, off:off + tm, :] * convw_ref[tap:tap + 1, c:c + MXU]
                y = term if y is None else y + term
            y = y * _sigmoid(y)
            if c < 2 * C_WIDTH:
                ss = _dot((y * y).astype(BF16), ones)
                y = y * lax.rsqrt(ss + EPS) * (HEAD_DIM ** -0.5 if c < C_WIDTH else 1.0)
            oc_ref[:, c:c + MXU] = y
        items.append((K1_OFF_C + c, MXU, finish, h_ext))

    for c in range(0, K1_Z, MXU):
        plain_item(K1_OFF_Z + c, MXU, oz_ref, c)
    plain_item(K1_OFF_BA, K1_BA, oba_ref, 0)
    n_conv = K1_C // MXU
    gap = len(items) // n_conv
    for j in range(n_conv):
        conv_item(j)
        items.insert(j * (gap + 1) + gap // 2, items.pop())

    product = lambda it: _dot(it[3] if len(it) > 3 else h, w_ref[:, it[0]:it[0] + it[1]])
    acc = product(items[0])
    for i, it in enumerate(items):
        nxt = product(items[i + 1]) if i + 1 < len(items) else None
        it[2](acc)
        acc = nxt


def _inproj(x2, n1, w, cos_t, sa_t, sb_t, gains, conv_w, batch, seq):
    tm = TOKEN_TILE
    t = x2.shape[0]
    ns = seq // tm
    row = lambda i: (i, 0)
    pos = lambda i: (i % ns, 0)
    d1, d2 = A_DILATIONS[1], A_DILATIONS[2]
    res = lambda i: (i // ns, 0, i % ns, 0)
    per = tm // IN_HALO
    last = t // IN_HALO - 1
    return pl.pallas_call(
        functools.partial(_inproj_kernel, tiles_per_seq=ns),
        name="inproj",
        grid=(t // tm,),
        in_specs=[
            pl.BlockSpec((tm, D_MODEL), row),
            pl.BlockSpec((IN_HALO, D_MODEL), lambda i: (jnp.maximum(i * per - 1, 0), 0)),
            pl.BlockSpec((IN_HALO, D_MODEL), lambda i: (jnp.minimum((i + 1) * per, last), 0)),
            _const_spec((1, D_MODEL)),
            _const_spec((D_MODEL, K1_WIDTH)),
            pl.BlockSpec((tm, LANES), pos),
            pl.BlockSpec((tm, LANES), pos),
            pl.BlockSpec((tm, LANES), pos),
            _const_spec((4, LANES)),
            _const_spec((C_CONV, K1_C)),
        ],
        out_specs=[
            pl.BlockSpec((tm, A_QKV), row),
            pl.BlockSpec((1, d1, tm // d1, A_QKV), res),
            pl.BlockSpec((1, d2, tm // d2, A_QKV), res),
            pl.BlockSpec((tm, K1_B), row),
            pl.BlockSpec((tm, K1_C), row),
            pl.BlockSpec((tm, K1_Z), row),
            pl.BlockSpec((tm, K1_BA), row),
        ],
        out_shape=[
            jax.ShapeDtypeStruct((t, A_QKV), BF16),
            jax.ShapeDtypeStruct((batch, d1, seq // d1, A_QKV), BF16),
            jax.ShapeDtypeStruct((batch, d2, seq // d2, A_QKV), BF16),
            jax.ShapeDtypeStruct((t, K1_B), BF16),
            jax.ShapeDtypeStruct((t, K1_C), F32),
            jax.ShapeDtypeStruct((t, K1_Z), F32),
            jax.ShapeDtypeStruct((t, K1_BA), F32),
        ],
        scratch_shapes=[pltpu.VMEM((4, tm, LANES), F32), pltpu.VMEM((2, tm + 2 * IN_HALO, MXU), F32)],
        compiler_params=_params("parallel"),
    )(x2, x2, x2, n1, w, cos_t, sa_t, sb_t, gains, conv_w)


def _attn_kernel(*refs, tq, halo, sub, length, kv_shared, has_sink, has_lse):
    refs = list(refs)
    sink_ref = refs.pop(0) if has_sink else None
    q_ref, kp_ref, kc_ref, kn_ref, vp_ref, vc_ref, vn_ref, o_ref = refs[:8]
    rest = refs[8:]
    lse_ref = rest.pop(0) if has_lse else None
    kbuf, vbuf = rest
    n = pl.program_id(2)
    nk = sub + 2 * halo

    kbuf[0:halo, :] = kp_ref[0, 0]
    kbuf[halo:halo + tq, :] = kc_ref[0, 0]
    kbuf[halo + tq:, :] = kn_ref[0, 0]
    vbuf[0:halo, :] = vp_ref[0, 0]
    vbuf[halo:halo + tq, :] = vc_ref[0, 0]
    vbuf[halo + tq:, :] = vn_ref[0, 0]

    rows = lax.broadcasted_iota(jnp.int32, (2 * sub, nk), 0) % sub
    cols = lax.broadcasted_iota(jnp.int32, (2 * sub, nk), 1)
    rel = cols - halo - rows
    band = (rel <= halo) & (rel >= -halo)
    lane_low = lax.broadcasted_iota(jnp.int32, (2 * sub, LANES), 1) < HEAD_DIM
    row_first = lax.broadcasted_iota(jnp.int32, (2 * sub, LANES), 0) < sub
    own_lanes = lane_low == row_first
    low_lane = lax.broadcasted_iota(jnp.int32, (sub, LANES), 1) < HEAD_DIM
    first_row = lax.broadcasted_iota(jnp.int32, (2 * sub, 1), 0) < sub

    for s in range(tq // sub):
        kpos = n * tq + (s * sub - halo) + cols
        mask = band & (kpos >= 0) & (kpos < length)
        qrows = slice(s * sub, (s + 1) * sub)
        krows = slice(s * sub, s * sub + nk)
        kcol = [(hp // 2 if kv_shared else hp) * LANES for hp in range(4)]
        sc = []
        for hp in range(4):
            qp = q_ref[0, 0, qrows, hp * LANES:(hp + 1) * LANES]
            qm = jnp.where(own_lanes, jnp.concatenate([qp, qp], axis=0), jnp.zeros((), qp.dtype))
            sc.append(jnp.where(mask, _dot_nt(qm, kbuf[krows, kcol[hp]:kcol[hp] + LANES]), NEG_INF))
        m = [jnp.max(x, axis=-1, keepdims=True) for x in sc]
        if has_sink:
            snk = [jnp.where(first_row, sink_ref[2 * hp], sink_ref[2 * hp + 1]) for hp in range(4)]
            m = [jnp.maximum(x, y) for x, y in zip(m, snk)]
        p = [jnp.exp2(x - y) for x, y in zip(sc, m)]
        den = [jnp.sum(x, axis=-1, keepdims=True) for x in p]
        if has_sink:
            den = [x + jnp.exp2(y - z) for x, y, z in zip(den, snk, m)]
        o = [_dot(x.astype(BF16), vbuf[krows, kc0:kc0 + LANES]) / y for x, kc0, y in zip(p, kcol, den)]
        for hp in range(4):
            lanes = slice(hp * LANES, (hp + 1) * LANES)
            o_ref[0, 0, qrows, lanes] = jnp.where(low_lane, o[hp][:sub], o[hp][sub:]).astype(o_ref.dtype)
            if has_lse:
                lse = jnp.broadcast_to(m[hp] + jnp.log2(den[hp]), (2 * sub, LANES))
                lse_ref[0, 0, qrows, lanes] = jnp.where(low_lane, lse[:sub], lse[sub:])


def _banded_attention(qkv, *, name, q_cb, k_cb, v_cb, kv_width, halo, tq, sub, kv_shared, sink=None,
                      has_lse=False, out_dtype=F32):
    batch, nres, length, _ = qkv.shape
    nb = length // tq
    per = tq // halo
    last = length // halo - 1
    q_spec = pl.BlockSpec((1, 1, tq, 512), lambda b, r, n: (b, r, n, q_cb))

    def kv_specs(cb):
        return [
            pl.BlockSpec((1, 1, halo, kv_width), lambda b, r, n: (b, r, jnp.maximum(n * per - 1, 0), cb)),
            pl.BlockSpec((1, 1, tq, kv_width), lambda b, r, n: (b, r, n, cb)),
            pl.BlockSpec((1, 1, halo, kv_width), lambda b, r, n: (b, r, jnp.minimum((n + 1) * per, last), cb)),
        ]

    in_specs = [q_spec] + kv_specs(k_cb) + kv_specs(v_cb)
    args = [qkv] * 7
    if sink is not None:
        in_specs = [pl.BlockSpec(memory_space=pltpu.SMEM)] + in_specs
        args = [sink] + args
    o_spec = pl.BlockSpec((1, 1, tq, 512), lambda b, r, n: (b, r, n, 0))
    out_specs = [o_spec]
    out_shape = [jax.ShapeDtypeStruct((batch, nres, length, 512), out_dtype)]
    if has_lse:
        out_specs.append(o_spec)
        out_shape.append(jax.ShapeDtypeStruct((batch, nres, length, 512), F32))
    kern = functools.partial(_attn_kernel, tq=tq, halo=halo, sub=sub, length=length, kv_shared=kv_shared,
                             has_sink=sink is not None, has_lse=has_lse)
    return pl.pallas_call(
        kern,
        name=name,
        grid=(batch, nres, nb),
        in_specs=in_specs,
        out_specs=out_specs,
        out_shape=out_shape,
        scratch_shapes=[pltpu.VMEM((tq + 2 * halo, kv_width), BF16), pltpu.VMEM((tq + 2 * halo, kv_width), BF16)],
        compiler_params=_params("parallel", "parallel", "parallel"),
    )(*args)


def _mixer_a(groups):
    res = []
    for g, qkv in enumerate(groups):
        res.append(_banded_attention(qkv, name=f"attn_a{g}", q_cb=0, k_cb=1, v_cb=2, kv_width=512, halo=A_HALO,
                                     tq=min(512, qkv.shape[2]), sub=128, kv_shared=False, has_lse=True))
    return res


def _mixer_b(qkv_b, sink, batch, seq):
    (o,) = _banded_attention(
        qkv_b.reshape(batch, 1, seq, K1_B), name="attn_b", q_cb=0, k_cb=2, v_cb=3, kv_width=256,
        halo=B_HALF_WINDOW, tq=512, sub=128, kv_shared=True, sink=sink, out_dtype=BF16)
    return o.reshape(batch * seq, 512)


C_TILE = 512
SCAN_TILE = 256
GROUP = 256
N_CHUNK = C_TILE // C_CHUNK
SCAN_CHUNKS = SCAN_TILE // C_CHUNK


def _head_mask():
    r = lax.broadcasted_iota(jnp.int32, (GROUP, GROUP), 0) // HEAD_DIM
    c = lax.broadcasted_iota(jnp.int32, (GROUP, GROUP), 1) // HEAD_DIM
    return r == c


def _blockdiag(x, headmask):
    xb = x.astype(BF16)
    return jnp.concatenate([xb, xb, xb, xb], axis=0) * headmask


def _delta_chunks_kernel(qkv_ref, ba_ref, alog_ref, dtb_ref,
                         uf_ref, wf_ref, qkf_ref, qgf_ref, kdf_ref, glf_ref,
                         ub_ref, wb_ref, qkb_ref, qgb_ref, kdb_ref, glb_ref,
                         beta_scr, gi_scr):
    outs = ((uf_ref, wf_ref, qkf_ref, qgf_ref, kdf_ref, glf_ref), (ub_ref, wb_ref, qkb_ref, qgb_ref, kdb_ref, glb_ref))
    ones = _head_ones(GROUP)

    ba = ba_ref[0]
    beta_all = _sigmoid(ba)
    zz = ba + dtb_ref[...]
    g_all = -jnp.exp(alog_ref[...]) * (jnp.maximum(zz, 0.0) + jnp.log(1.0 + jnp.exp(-jnp.abs(zz))))
    lane_head = lax.broadcasted_iota(jnp.int32, (LANES, 512), 1) // HEAD_DIM
    krow = lax.broadcasted_iota(jnp.int32, (LANES, 512), 0)
    ti = lax.broadcasted_iota(jnp.int32, (MXU, MXU), 0)
    tj = lax.broadcasted_iota(jnp.int32, (MXU, MXU), 1)
    same_chunk = (ti // C_CHUNK) == (tj // C_CHUNK)
    for d in range(2):
        e_beta = jnp.where(krow == lane_head + d * C_HEADS, 1.0, 0.0).astype(BF16)
        e_g = jnp.where(krow == lane_head + (2 + d) * C_HEADS, 1.0, 0.0).astype(BF16)
        beta_scr[d] = _dot_unit_sel(beta_all, e_beta)
        tri = jnp.where(same_chunk & ((tj <= ti) if d == 0 else (tj >= ti)), 1.0, 0.0).astype(BF16)
        for r in range(0, C_TILE, MXU):
            gi_scr[d, r:r + MXU, :] = _dot_data_sel(_dot_sel_data(tri, g_all[r:r + MXU]), e_g)

    ii = lax.broadcasted_iota(jnp.int32, (C_CHUNK, 512), 0)
    jj = lax.broadcasted_iota(jnp.int32, (C_CHUNK, 512), 1) % C_CHUNK
    ig = lax.broadcasted_iota(jnp.int32, (C_CHUNK, GROUP), 0)
    jg = lax.broadcasted_iota(jnp.int32, (C_CHUNK, GROUP), 1) % C_CHUNK
    incl = (ii >= jj, ii <= jj)
    incl_g = (ig >= jg, ig <= jg)
    strict_g = (ig > jg, ig < jg)
    eye = jnp.where(ig == jg, 1.0, 0.0)
    diag16 = (ig // 16) == (jg // 16)
    headmask = ones
    zero_rows = jnp.zeros((LANES - C_CHUNK, LANES), F32)

    def bd(qs):
        return [_blockdiag(q, headmask) for q in qs]

    def mm(ps, bds):
        return [_dot(p.astype(BF16), b) for p, b in zip(ps, bds)]

    def mm2(ps, rs, bds):
        both = [_dot(jnp.concatenate([p, r], axis=0).astype(BF16), b) for p, r, b in zip(ps, rs, bds)]
        return [x[:C_CHUNK] for x in both], [x[C_CHUNK:] for x in both]

    def add(xs, ys):
        return [x + y for x, y in zip(xs, ys)]

    def unit_inverses(ls):
        dg = [jnp.where(diag16, l, 0.0) for l in ls]
        og = [l - d for l, d in zip(ls, dg)]
        di = [eye - d for d in dg]
        d2 = mm(dg, bd(dg))
        d4, t = mm2(d2, di, bd(d2))
        di = add(di, t)
        d8, t = mm2(d4, di, bd(d4))
        di = add(di, t)
        di = add(di, mm(di, bd(d8)))
        nn = mm(di, bd(og))
        m = [eye - x for x in nn]
        nbd = bd(nn)
        m = add(m, mm(mm(m, nbd), nbd))
        return mm(m, bd(di))

    def transposed_decay(gi_):
        parts = []
        for p in range(512 // LANES):
            t = jnp.concatenate([gi_[:, p * LANES:(p + 1) * LANES], zero_rows], axis=0).T
            parts.append(t[:C_CHUNK, :] + pltpu.roll(t[C_CHUNK:, :], HEAD_DIM, 1))
        return jnp.concatenate(parts, axis=1)

    def chunks(cs):
        lms, vbs, kbgs, dest = [], [], [], []
        for c in cs:
            rows = pl.ds(pl.multiple_of(c * C_CHUNK, C_CHUNK), C_CHUNK)
            qc, kc, vc = (qkv_ref[0, rows, i * C_WIDTH:(i + 1) * C_WIDTH] for i in range(3))
            kk, qk = [], []
            for gidx in range(2):
                sl = slice(gidx * GROUP, (gidx + 1) * GROUP)
                kbd = _blockdiag(kc[:, sl], headmask)
                kk.append(_dot_nt(kc[:, sl].astype(BF16), kbd))
                qk.append(_dot_nt(qc[:, sl].astype(BF16), kbd))
            for d in range(2):
                u_ref, w_ref, qk_ref, qg_ref, kd_ref, gl_ref = outs[d]
                beta = beta_scr[d, rows, :]
                gi_ = gi_scr[d, rows, :]
                decay = jnp.exp(jnp.where(incl[d], gi_ - transposed_decay(gi_), NEG_INF))
                eg = jnp.exp(gi_)
                last = C_CHUNK - 1 if d == 0 else 0
                glrow = gi_[last:last + 1, :]
                kd_ref[0, rows, :] = (kc * jnp.exp(glrow - gi_)).astype(BF16)
                qg_ref[0, rows, :] = (qc * eg).astype(BF16)
                gl_ref[0, pl.ds(c, 1), 0, :] = jnp.exp(glrow)
                kbg = kc * beta * eg
                vb = vc * beta
                for gidx in range(2):
                    sl = slice(gidx * GROUP, (gidx + 1) * GROUP)
                    lms.append(jnp.where(strict_g[d], kk[gidx] * beta[:, sl] * decay[:, sl], 0.0))
                    qk_ref[0, rows, sl] = jnp.where(incl_g[d], qk[gidx] * decay[:, sl], 0.0).astype(BF16)
                    vbs.append(vb[:, sl])
                    kbgs.append(kbg[:, sl])
                    dest.append((u_ref, w_ref, rows, sl))
        tinvs = unit_inverses(lms)
        uw = mm(tinvs, [jnp.concatenate([a, b], axis=1) for a, b in zip(bd(vbs), bd(kbgs))])
        for (u_ref, w_ref, rows, sl), x in zip(dest, uw):
            u_ref[0, rows, sl] = x[:, :GROUP]
            w_ref[0, rows, sl] = x[:, GROUP:].astype(BF16)

    def body(c2, carry):
        chunks((2 * c2, 2 * c2 + 1))
        return carry

    lax.fori_loop(0, N_CHUNK // 2, body, 0)


def _delta_chunks(qkv, ba, alog_row, dtb_row):
    batch, seq, _ = qkv.shape
    nt = seq // C_TILE
    tile = lambda b, n: (b, n, 0)
    wide = lambda dt: jax.ShapeDtypeStruct((batch, seq, C_WIDTH), dt)
    gl_sds = jax.ShapeDtypeStruct((batch, seq // C_CHUNK, 1, C_WIDTH), F32)
    out_shape = [wide(F32), wide(BF16), wide(BF16), wide(BF16), wide(BF16), gl_sds] * 2
    wide_spec = pl.BlockSpec((1, C_TILE, C_WIDTH), tile)
    gl_spec = pl.BlockSpec((1, N_CHUNK, 1, C_WIDTH), lambda b, n: (b, n, 0, 0))
    return pl.pallas_call(
        _delta_chunks_kernel,
        name="delta_chunks",
        grid=(batch, nt),
        in_specs=[
            pl.BlockSpec((1, C_TILE, K1_C), tile),
            pl.BlockSpec((1, C_TILE, K1_BA), tile),
            _const_spec((1, LANES)), _const_spec((1, LANES))],
        out_specs=[wide_spec] * 5 + [gl_spec] + [wide_spec] * 5 + [gl_spec],
        out_shape=out_shape,
        scratch_shapes=[
            pltpu.VMEM((2, C_TILE, C_WIDTH), F32),
            pltpu.VMEM((2, C_TILE, C_WIDTH), F32),
        ],
        compiler_params=_params("parallel", "parallel"),
    )(qkv, ba, alog_row, dtb_row)


def _delta_scan_kernel(uf_ref, wf_ref, qkf_ref, qgf_ref, kdf_ref, glf_ref,
                       ub_ref, wb_ref, qkb_ref, qgb_ref, kdb_ref, glb_ref, of_ref, ob_ref, state_scr):
    n = pl.program_id(1)
    bblk = uf_ref.shape[0]
    ins = ((uf_ref, wf_ref, qkf_ref, qgf_ref, kdf_ref, glf_ref, of_ref),
           (ub_ref, wb_ref, qkb_ref, qgb_ref, kdb_ref, glb_ref, ob_ref))

    @pl.when(n == 0)
    def _():
        state_scr[...] = jnp.zeros_like(state_scr)

    headmask = _head_mask()
    head01 = _head_ones(GROUP)

    def body(c, carry):
        chains = []
        for d in range(2):
            cc = c if d == 0 else SCAN_CHUNKS - 1 - c
            rows = pl.ds(pl.multiple_of(cc * C_CHUNK, C_CHUNK), C_CHUNK)
            for b in range(bblk):
                for gidx in range(2):
                    chains.append((d, b, gidx, cc, rows, slice(gidx * GROUP, (gidx + 1) * GROUP)))
        states = [state_scr[b, d, gidx] for d, b, gidx, _, _, _ in chains]
        both = [_dot(jnp.concatenate([ins[d][1][b, rows, sl], ins[d][3][b, rows, sl]], axis=0), st.astype(BF16))
                for (d, b, _, _, rows, sl), st in zip(chains, states)]
        v_new = [ins[d][0][b, rows, sl] - bo[:C_CHUNK] for (d, b, _, _, rows, sl), bo in zip(chains, both)]
        intra = [_dot(ins[d][2][b, rows, sl], _blockdiag(v, head01)) for (d, b, _, _, rows, sl), v in zip(chains, v_new)]
        upd = [_dot_tn(ins[d][4][b, rows, sl], v.astype(BF16)) for (d, b, _, _, rows, sl), v in zip(chains, v_new)]
        for (d, b, gidx, cc, rows, sl), st, bo, it, up in zip(chains, states, both, intra, upd):
            ins[d][6][b, rows, sl] = bo[C_CHUNK:] + it
            gl = ins[d][5][b, pl.ds(cc, 1), 0, :]
            state_scr[b, d, gidx] = st * gl[:, sl] + jnp.where(headmask, up, 0.0)
        return carry

    lax.fori_loop(0, SCAN_CHUNKS, body, 0)


def _delta_scan(factors):
    batch, seq, _ = factors[0].shape
    nt = seq // SCAN_TILE
    bblk = min(batch, 4)
    fwd = lambda b, n: (b, n, 0)
    bwd = lambda b, n: (b, nt - 1 - n, 0)

    def specs(tile):
        wide = pl.BlockSpec((bblk, SCAN_TILE, C_WIDTH), tile)
        gl = pl.BlockSpec((bblk, SCAN_CHUNKS, 1, C_WIDTH), lambda b, n: tile(b, n) + (0,))
        return [wide] * 5 + [gl]

    out_sds = jax.ShapeDtypeStruct((batch, seq, C_WIDTH), F32)
    return pl.pallas_call(
        _delta_scan_kernel,
        name="delta_scan",
        grid=(batch // bblk, nt),
        in_specs=specs(fwd) + specs(bwd),
        out_specs=[pl.BlockSpec((bblk, SCAN_TILE, C_WIDTH), fwd), pl.BlockSpec((bblk, SCAN_TILE, C_WIDTH), bwd)],
        out_shape=[out_sds, out_sds],
        scratch_shapes=[pltpu.VMEM((bblk, 2, 2, GROUP, GROUP), F32)],
        compiler_params=_params("parallel", "arbitrary"),
    )(*factors)


def _merge_kernel(x_ref, n1_ref, oa0, la0, oa1, la1, oa2, la2, ob_ref, ocf_ref, ocb_ref, z_ref, ogain_ref,
                  wg_ref, wba_ref, wbb_ref, wbc_ref, wo_ref, out_ref, inter_scr):
    tm = x_ref.shape[0]
    x = x_ref[...]
    h = _rms_rows(x, n1_ref[...]).astype(BF16)

    def interleaved(ref, d, base):
        for r in range(d):
            for c in range(512 // LANES):
                inter_scr[base + c, pl.ds(r, tm // d, stride=d), :] = ref[0, r, :, c * LANES:(c + 1) * LANES]
        return jnp.concatenate([inter_scr[base + c] for c in range(512 // LANES)], axis=1)

    def branch(br, o_br, w_ref):
        gate = _sigmoid(_dot(h, wg_ref[:, br * D_MODEL:(br + 1) * D_MODEL]))
        return gate * _dot(o_br.astype(BF16), w_ref[...])

    term_b = branch(1, ob_ref[...], wbb_ref)

    d1, d2 = A_DILATIONS[1], A_DILATIONS[2]
    l0, o0 = la0[...], oa0[...]
    l1, o1 = interleaved(la1, d1, 0), interleaved(oa1, d1, 4)
    l2, o2 = interleaved(la2, d2, 8), interleaved(oa2, d2, 12)
    m = jnp.maximum(jnp.maximum(l0, l1), l2)
    e0, e1, e2 = jnp.exp2(l0 - m), jnp.exp2(l1 - m), jnp.exp2(l2 - m)
    o_a = (e0 * o0 + e1 * o1 + e2 * o2) / (e0 + e1 + e2)
    merged = branch(0, o_a, wba_ref) + term_b
    oc = ocf_ref[...] + ocb_ref[...]
    ones = _head_ones(MXU)
    z = z_ref[...]
    parts = []
    for c in range(0, C_WIDTH, MXU):
        blk = oc[:, c:c + MXU]
        ss = _dot((blk * blk).astype(BF16), ones)
        zz = z[:, c:c + MXU]
        parts.append(blk * lax.rsqrt(ss * (1.0 / HEAD_DIM) + EPS) * ogain_ref[:, c:c + MXU] * (zz * _sigmoid(zz)))
    o_c = jnp.concatenate(parts, axis=1)
    merged = merged + branch(2, o_c, wbc_ref)
    out_ref[...] = x + _dot(merged.astype(BF16), wo_ref[...])


def _merge(x2, n1, a_res, o_b, o_cf, o_cb, z, ogain, wg, wba, wbb, wbc, wo, seq):
    tm = TOKEN_TILE
    t = x2.shape[0]
    ns = seq // tm
    row = lambda i: (i, 0)
    res = lambda i: (i // ns, 0, i % ns, 0)
    wide = pl.BlockSpec((tm, D_MODEL), row)
    half = pl.BlockSpec((tm, 512), row)
    d1, d2 = A_DILATIONS[1], A_DILATIONS[2]
    res1 = pl.BlockSpec((1, d1, tm // d1, 512), res)
    res2 = pl.BlockSpec((1, d2, tm // d2, 512), res)
    (o0, l0), (o1, l1), (o2, l2) = a_res
    return pl.pallas_call(
        _merge_kernel,
        name="merge",
        grid=(t // tm,),
        in_specs=[wide, _const_spec((1, D_MODEL)), half, half, res1, res1, res2, res2] + [half] * 4 + [
            _const_spec((1, C_WIDTH)), _const_spec((D_MODEL, N_BRANCH * D_MODEL)), _const_spec((512, D_MODEL)),
            _const_spec((512, D_MODEL)), _const_spec((512, D_MODEL)), _const_spec((D_MODEL, D_MODEL))],
        out_specs=wide,
        out_shape=jax.ShapeDtypeStruct((t, D_MODEL), F32),
        scratch_shapes=[pltpu.VMEM((16, tm, LANES), F32)],
        compiler_params=_params("parallel"),
    )(x2, n1, o0.reshape(t, 512), l0.reshape(t, 512), o1, l1, o2, l2, o_b, o_cf, o_cb, z, ogain, wg, wba, wbb, wbc, wo)


def _ffn_kernel(x_ref, n2_ref, wi_ref, wo_ref, out_ref):
    x = x_ref[...]
    h = _rms_rows(x, n2_ref[...]).astype(BF16)
    acc = x
    for c in range(0, FFN_HIDDEN, MXU):
        gate = _dot(h, wi_ref[:, c:c + MXU])
        up = _dot(h, wi_ref[:, FFN_HIDDEN + c:FFN_HIDDEN + c + MXU])
        act = (gate * _sigmoid(gate) * up).astype(BF16)
        acc = acc + _dot(act, wo_ref[c:c + MXU, :])
    out_ref[...] = acc


def _ffn(x2, n2, wi, wo):
    tm = TOKEN_TILE
    t = x2.shape[0]
    row = lambda i: (i, 0)
    return pl.pallas_call(
        _ffn_kernel,
        name="ffn",
        grid=(t // tm,),
        in_specs=[pl.BlockSpec((tm, D_MODEL), row), _const_spec((1, D_MODEL)),
                  _const_spec((D_MODEL, 2 * FFN_HIDDEN)), _const_spec((FFN_HIDDEN, D_MODEL))],
        out_specs=pl.BlockSpec((tm, D_MODEL), row),
        out_shape=jax.ShapeDtypeStruct((t, D_MODEL), F32),
        compiler_params=_params("parallel"),
    )(x2, n2, wi, wo)


def _rope_tables(seq):
    pos = jnp.arange(seq, dtype=F32)
    inv_freq = jnp.power(jnp.float32(ROPE_THETA), -jnp.arange(0, ROPE_DIM, 2, dtype=F32) / ROPE_DIM)
    ang = pos[:, None] * inv_freq[None, :]
    cos, sin = jnp.cos(ang), jnp.sin(ang)
    half = ROPE_DIM // 2
    one = jnp.ones((seq, HEAD_DIM - ROPE_DIM), F32)
    zero = jnp.zeros((seq, HEAD_DIM - half), F32)
    cos_h = jnp.concatenate([cos, cos, one], axis=1)
    sa_h = jnp.concatenate([-sin, zero], axis=1)
    sb_h = jnp.concatenate([jnp.zeros((seq, half), F32), sin, zero[:, half:]], axis=1)
    tile = lambda t: jnp.concatenate([t, t], axis=1)
    return tile(cos_h), tile(sa_h), tile(sb_h)


def _pack_layer(l, norm1, w_in, qk_gain, sink, conv_w, a_log, dt_bias, o_gain, w_gate, w_br_a, w_br_b, w_br_c,
                w_out, norm2, w_ffn_in, w_ffn_out):
    wi = w_in[l]
    kb0, kb1 = wi[:, 5120:5184], wi[:, 5184:5248]
    vb0, vb1 = wi[:, 5248:5312], wi[:, 5312:5376]
    w1 = jnp.concatenate([
        wi[:, :5120], kb0, kb0, kb1, kb1, vb0, vb0, vb1, vb1, wi[:, 5376:7424],
        wi[:, 7424:7456], jnp.zeros((D_MODEL, K1_BA - 32), F32)], axis=1).astype(BF16)
    scale = HEAD_DIM ** -0.5 * LOG2_E
    gains = jnp.stack([qk_gain[l, 0] * scale, qk_gain[l, 1], qk_gain[l, 2] * scale, qk_gain[l, 3]])
    gains = jnp.concatenate([gains, gains], axis=1)
    pad = lambda v: jnp.concatenate([jnp.zeros((16,), F32), v.reshape(-1), jnp.zeros((LANES - 32,), F32)])[None, :]
    return dict(
        n1=norm1[l][None, :], w1=w1, gains=gains, sink=sink[l] * LOG2_E, conv_w=conv_w[l],
        alog=pad(a_log[l]), dtb=pad(dt_bias[l]), ogain=jnp.tile(o_gain[l], C_HEADS)[None, :],
        wg=w_gate[l].astype(BF16), wba=w_br_a[l].astype(BF16), wbb=w_br_b[l].astype(BF16),
        wbc=w_br_c[l].astype(BF16), wo=w_out[l].astype(BF16), n2=norm2[l][None, :],
        wfi=w_ffn_in[l].astype(BF16), wfo=w_ffn_out[l].astype(BF16))


def _layer(x2, p, tabs, batch, seq):
    t = batch * seq
    a0, a1, a2, qkv_b, qkv_c, z, ba = _inproj(x2, p["n1"], p["w1"], *tabs, p["gains"], p["conv_w"], batch, seq)
    a_res = _mixer_a((a0.reshape(batch, 1, seq, A_QKV), a1, a2))
    o_b = _mixer_b(qkv_b, p["sink"], batch, seq)
    factors = _delta_chunks(qkv_c.reshape(batch, seq, K1_C), ba.reshape(batch, seq, K1_BA), p["alog"], p["dtb"])
    o_cf, o_cb = _delta_scan(factors)
    x2 = _merge(x2, p["n1"], a_res, o_b, o_cf.reshape(t, C_WIDTH), o_cb.reshape(t, C_WIDTH), z, p["ogain"],
                p["wg"], p["wba"], p["wbb"], p["wbc"], p["wo"], seq)
    return _ffn(x2, p["n2"], p["wfi"], p["wfo"])


def kernel(x_prompt, x_sample, norm1, w_in, qk_gain, sink, conv_w, a_log, dt_bias, o_gain, w_gate, w_br_a, w_br_b,
           w_br_c, w_out, norm2, w_ffn_in, w_ffn_out):
    depth = w_in.shape[0]
    layers = [_pack_layer(l, norm1, w_in, qk_gain, sink, conv_w, a_log, dt_bias, o_gain, w_gate, w_br_a, w_br_b,
                          w_br_c, w_out, norm2, w_ffn_in, w_ffn_out) for l in range(depth)]
    outs = []
    for x in (x_prompt, x_sample):
        batch, seq, _ = x.shape
        assert seq % (A_DILATIONS[-1] * 256) == 0
        tabs = _rope_tables(seq)
        x2 = x.reshape(batch * seq, D_MODEL)
        for p in layers:
            x2 = _layer(x2, p, tabs, batch, seq)
        outs.append(x2.reshape(batch, seq, D_MODEL))
    return tuple(outs)
```

```python
import functools

import jax
import jax.numpy as jnp
from jax import lax
from jax.experimental import pallas as pl
from jax.experimental.pallas import tpu as pltpu

F32 = jnp.float32
BF16 = jnp.bfloat16

D_MODEL = 1024
HEAD_DIM = 64
ROPE_DIM = HEAD_DIM // 4
ROPE_THETA = 500000.0
EPS = 1e-6
NEG_INF = -1e30
LOG2_E = 1.4426950408889634

A_PATTERNS = ((128, 1), (512, 4), (2048, 16))
A_DILATIONS = tuple(d for _, d in A_PATTERNS)
A_GROUPS = 3
A_WIDTH = 512
A_QKV = A_GROUPS * A_WIDTH
A_HALO = 64
B_HALF_WINDOW = 128
C_HEADS = 8
C_CHUNK = 64
C_CONV = 5
C_WIDTH = 512
FFN_HIDDEN = 2816
N_BRANCH = 3

LANES = 128
MXU = 256
VMEM_LIMIT = 56 * 1024 * 1024
TOKEN_TILE = 512
IN_HALO = 16

K1_A = 3 * A_QKV
K1_B = 512 + 256 + 256
K1_C = 3 * C_WIDTH
K1_Z = C_WIDTH
K1_BA = LANES
K1_OFF_B = K1_A
K1_OFF_C = K1_OFF_B + K1_B
K1_OFF_Z = K1_OFF_C + K1_C
K1_OFF_BA = K1_OFF_Z + K1_Z
K1_WIDTH = K1_OFF_BA + K1_BA


def _dot(a, b):
    return jnp.dot(a, b, preferred_element_type=F32)


def _dot_nt(a, b):
    return lax.dot_general(a, b, (((1,), (1,)), ((), ())), preferred_element_type=F32)


def _dot_tn(a, b):
    return lax.dot_general(a, b, (((0,), (0,)), ((), ())), preferred_element_type=F32)


def _split3(a):
    a1 = a.astype(BF16)
    r1 = a - a1.astype(F32)
    a2 = r1.astype(BF16)
    a3 = (r1 - a2.astype(F32)).astype(BF16)
    return a1, a2, a3


def _dot_data_sel(a, sel):
    a1, a2, a3 = _split3(a)
    return _dot(a1, sel) + _dot(a2, sel) + _dot(a3, sel)


def _dot_unit_sel(a, sel):
    a1, a2, _ = _split3(a)
    return _dot(a1, sel) + _dot(a2, sel)


def _dot_sel_data(sel, b):
    b1, b2, b3 = _split3(b)
    return _dot(sel, b1) + _dot(sel, b2) + _dot(sel, b3)


def _rms_rows(x, gain):
    ms = jnp.mean(x * x, axis=-1, keepdims=True)
    return x * lax.rsqrt(ms + EPS) * gain


def _sigmoid(x):
    return 0.5 * jnp.tanh(0.5 * x) + 0.5


def _head_ones(n):
    r = lax.broadcasted_iota(jnp.int32, (n, n), 0) // HEAD_DIM
    c = lax.broadcasted_iota(jnp.int32, (n, n), 1) // HEAD_DIM
    return jnp.where(r == c, 1.0, 0.0).astype(BF16)


def _const_spec(shape):
    nd = len(shape)
    return pl.BlockSpec(shape, lambda *_: (0,) * nd, pipeline_mode=pl.Buffered(1))


def _params(*sem):
    return pltpu.CompilerParams(dimension_semantics=sem, vmem_limit_bytes=VMEM_LIMIT)


def _inproj_kernel(x_ref, xp_ref, xn_ref, n1_ref, w_ref, cos_ref, sa_ref, sb_ref, gain_ref, convw_ref,
                   a0_ref, a1_ref, a2_ref, ob_ref, oc_ref, oz_ref, oba_ref, deint_scr, conv_scr, *, tiles_per_seq):
    tm = x_ref.shape[0]
    x_ext = jnp.concatenate([xp_ref[...], x_ref[...], xn_ref[...]], axis=0)
    h_ext = _rms_rows(x_ext, n1_ref[...]).astype(BF16)
    h = h_ext[IN_HALO:IN_HALO + tm]
    ones = _head_ones(MXU)
    tile_in_seq = pl.program_id(0) % tiles_per_seq
    ext_row = lax.broadcasted_iota(jnp.int32, (tm + 2 * IN_HALO, 1), 0)
    in_seq = (((ext_row >= IN_HALO) | (tile_in_seq > 0))
              & ((ext_row < IN_HALO + tm) | (tile_in_seq < tiles_per_seq - 1)))
    cos, sa, sb = cos_ref[...], sa_ref[...], sb_ref[...]
    a_refs = (a0_ref, a1_ref, a2_ref)
    slot = [0]

    def qk_norm_rope(acc, gain):
        ss = _dot((acc * acc).astype(BF16), ones)
        r = lax.rsqrt(ss * (1.0 / HEAD_DIM) + EPS)
        outs = []
        for s in range(0, MXU, LANES):
            y = acc[:, s:s + LANES] * r[:, s:s + LANES] * gain
            outs.append(y * cos + pltpu.roll(y, LANES - 8, 1) * sa + pltpu.roll(y, 8, 1) * sb)
        return outs

    def emit_a(g, col, val):
        d = A_DILATIONS[g]
        if d == 1:
            a0_ref[:, col:col + LANES] = val.astype(BF16)
            return
        s = slot[0]
        slot[0] = (s + 1) % deint_scr.shape[0]
        deint_scr[s] = val
        for r in range(d):
            a_refs[g][0, r, :, col:col + LANES] = deint_scr[s, pl.ds(r, tm // d, stride=d), :].astype(BF16)

    items = []

    def a_item(kind, g, c):
        def finish(acc):
            parts = qk_norm_rope(acc, gain_ref[kind:kind + 1, :]) if kind < 2 else [acc[:, :LANES], acc[:, LANES:]]
            for i, part in enumerate(parts):
                emit_a(g, kind * A_WIDTH + c + i * LANES, part)
        items.append((kind * A_QKV + g * A_WIDTH + c, MXU, finish))

    def qk_item(c0, gain_row, out_ref, o0):
        def finish(acc):
            for i, part in enumerate(qk_norm_rope(acc, gain_ref[gain_row:gain_row + 1, :])):
                out_ref[:, o0 + i * LANES:o0 + (i + 1) * LANES] = part.astype(out_ref.dtype)
        items.append((c0, MXU, finish))

    def plain_item(c0, width, out_ref, o0):
        def finish(acc):
            out_ref[:, o0:o0 + width] = acc.astype(out_ref.dtype)
        items.append((c0, width, finish))

    for kind in range(3):
        for g in range(A_GROUPS):
            for c in range(0, A_WIDTH, MXU):
                a_item(kind, g, c)
    for c in range(0, 512, MXU):
        qk_item(K1_OFF_B + c, 2, ob_ref, c)
    qk_item(K1_OFF_B + 512, 3, ob_ref, 512)
    plain_item(K1_OFF_B + 768, 256, ob_ref, 768)
    def conv_item(j):
        c = j * MXU

        def finish(acc):
            buf = j % conv_scr.shape[0]
            conv_scr[buf] = jnp.where(in_seq, acc, 0.0)
            y = None
            for tap in range(C_CONV):
                off = IN_HALO - C_CONV // 2 + tap
                term = conv_scr[buf, off:off + tm, :] * convw_ref[tap:tap + 1, c:c + MXU]
                y = term if y is None else y + term
            y = y * _sigmoid(y)
            if c < 2 * C_WIDTH:
                ss = _dot((y * y).astype(BF16), ones)
                y = y * lax.rsqrt(ss + EPS) * (HEAD_DIM ** -0.5 if c < C_WIDTH else 1.0)
            oc_ref[:, c:c + MXU] = y
        items.append((K1_OFF_C + c, MXU, finish, h_ext))

    for c in range(0, K1_Z, MXU):
        plain_item(K1_OFF_Z + c, MXU, oz_ref, c)
    plain_item(K1_OFF_BA, K1_BA, oba_ref, 0)
    n_conv = K1_C // MXU
    gap = len(items) // n_conv
    for j in range(n_conv):
        conv_item(j)
        items.insert(j * (gap + 1) + gap // 2, items.pop())

    product = lambda it: _dot(it[3] if len(it) > 3 else h, w_ref[:, it[0]:it[0] + it[1]])
    pending, issued = {}, 0
    for i, it in enumerate(items):
        while issued <= min(i + (2 if len(it) > 3 else 1), len(items) - 1):
            pending[issued] = product(items[issued])
            issued += 1
        it[2](pending.pop(i))


def _inproj(x2, n1, w, cos_t, sa_t, sb_t, gains, conv_w, batch, seq):
    tm = TOKEN_TILE
    t = x2.shape[0]
    ns = seq // tm
    row = lambda i: (i, 0)
    pos = lambda i: (i % ns, 0)
    d1, d2 = A_DILATIONS[1], A_DILATIONS[2]
    res = lambda i: (i // ns, 0, i % ns, 0)
    per = tm // IN_HALO
    last = t // IN_HALO - 1
    return pl.pallas_call(
        functools.partial(_inproj_kernel, tiles_per_seq=ns),
        name="inproj",
        grid=(t // tm,),
        in_specs=[
            pl.BlockSpec((tm, D_MODEL), row),
            pl.BlockSpec((IN_HALO, D_MODEL), lambda i: (jnp.maximum(i * per - 1, 0), 0)),
            pl.BlockSpec((IN_HALO, D_MODEL), lambda i: (jnp.minimum((i + 1) * per, last), 0)),
            _const_spec((1, D_MODEL)),
            _const_spec((D_MODEL, K1_WIDTH)),
            pl.BlockSpec((tm, LANES), pos),
            pl.BlockSpec((tm, LANES), pos),
            pl.BlockSpec((tm, LANES), pos),
            _const_spec((4, LANES)),
            _const_spec((C_CONV, K1_C)),
        ],
        out_specs=[
            pl.BlockSpec((tm, A_QKV), row),
            pl.BlockSpec((1, d1, tm // d1, A_QKV), res),
            pl.BlockSpec((1, d2, tm // d2, A_QKV), res),
            pl.BlockSpec((tm, K1_B), row),
            pl.BlockSpec((tm, K1_C), row),
            pl.BlockSpec((tm, K1_Z), row),
            pl.BlockSpec((tm, K1_BA), row),
        ],
        out_shape=[
            jax.ShapeDtypeStruct((t, A_QKV), BF16),
            jax.ShapeDtypeStruct((batch, d1, seq // d1, A_QKV), BF16),
            jax.ShapeDtypeStruct((batch, d2, seq // d2, A_QKV), BF16),
            jax.ShapeDtypeStruct((t, K1_B), BF16),
            jax.ShapeDtypeStruct((t, K1_C), F32),
            jax.ShapeDtypeStruct((t, K1_Z), F32),
            jax.ShapeDtypeStruct((t, K1_BA), F32),
        ],
        scratch_shapes=[pltpu.VMEM((4, tm, LANES), F32), pltpu.VMEM((2, tm + 2 * IN_HALO, MXU), F32)],
        compiler_params=_params("parallel"),
    )(x2, x2, x2, n1, w, cos_t, sa_t, sb_t, gains, conv_w)


def _attn_kernel(*refs, tq, halo, sub, length, kv_shared, has_sink, has_lse):
    refs = list(refs)
    sink_ref = refs.pop(0) if has_sink else None
    q_ref, kp_ref, kc_ref, kn_ref, vp_ref, vc_ref, vn_ref, o_ref = refs[:8]
    rest = refs[8:]
    lse_ref = rest.pop(0) if has_lse else None
    kbuf, vbuf = rest
    n = pl.program_id(2)
    nk = sub + 2 * halo

    kbuf[0:halo, :] = kp_ref[0, 0]
    kbuf[halo:halo + tq, :] = kc_ref[0, 0]
    kbuf[halo + tq:, :] = kn_ref[0, 0]
    vbuf[0:halo, :] = vp_ref[0, 0]
    vbuf[halo:halo + tq, :] = vc_ref[0, 0]
    vbuf[halo + tq:, :] = vn_ref[0, 0]

    rows = lax.broadcasted_iota(jnp.int32, (2 * sub, nk), 0) % sub
    cols = lax.broadcasted_iota(jnp.int32, (2 * sub, nk), 1)
    rel = cols - halo - rows
    band = (rel <= halo) & (rel >= -halo)
    lane_low = lax.broadcasted_iota(jnp.int32, (2 * sub, LANES), 1) < HEAD_DIM
    row_first = lax.broadcasted_iota(jnp.int32, (2 * sub, LANES), 0) < sub
    own_lanes = lane_low == row_first
    low_lane = lax.broadcasted_iota(jnp.int32, (sub, LANES), 1) < HEAD_DIM
    first_row = lax.broadcasted_iota(jnp.int32, (2 * sub, 1), 0) < sub

    for s in range(tq // sub):
        kpos = n * tq + (s * sub - halo) + cols
        mask = band & (kpos >= 0) & (kpos < length)
        qrows = slice(s * sub, (s + 1) * sub)
        krows = slice(s * sub, s * sub + nk)
        kcol = [(hp // 2 if kv_shared else hp) * LANES for hp in range(4)]
        sc = []
        for hp in range(4):
            qp = q_ref[0, 0, qrows, hp * LANES:(hp + 1) * LANES]
            qm = jnp.where(own_lanes, jnp.concatenate([qp, qp], axis=0), jnp.zeros((), qp.dtype))
            sc.append(jnp.where(mask, _dot_nt(qm, kbuf[krows, kcol[hp]:kcol[hp] + LANES]), NEG_INF))
        m = [jnp.max(x, axis=-1, keepdims=True) for x in sc]
        if has_sink:
            snk = [jnp.where(first_row, sink_ref[2 * hp], sink_ref[2 * hp + 1]) for hp in range(4)]
            m = [jnp.maximum(x, y) for x, y in zip(m, snk)]
        p = [jnp.exp2(x - y) for x, y in zip(sc, m)]
        den = [jnp.sum(x, axis=-1, keepdims=True) for x in p]
        if has_sink:
            den = [x + jnp.exp2(y - z) for x, y, z in zip(den, snk, m)]
        o = [_dot(x.astype(BF16), vbuf[krows, kc0:kc0 + LANES]) for x, kc0 in zip(p, kcol)]
        for hp in range(4):
            lanes = slice(hp * LANES, (hp + 1) * LANES)
            den_p = jnp.where(low_lane, den[hp][:sub], den[hp][sub:])
            o_p = jnp.where(low_lane, o[hp][:sub], o[hp][sub:]) / den_p
            o_ref[0, 0, qrows, lanes] = o_p.astype(o_ref.dtype)
            if has_lse:
                m_p = jnp.where(low_lane, m[hp][:sub], m[hp][sub:])
                lse_ref[0, 0, qrows, lanes] = m_p + jnp.log2(den_p)


def _banded_attention(qkv, *, name, q_cb, k_cb, v_cb, kv_width, halo, tq, sub, kv_shared, sink=None,
                      has_lse=False, out_dtype=F32):
    batch, nres, length, _ = qkv.shape
    nb = length // tq
    per = tq // halo
    last = length // halo - 1
    q_spec = pl.BlockSpec((1, 1, tq, 512), lambda b, r, n: (b, r, n, q_cb))

    def kv_specs(cb):
        return [
            pl.BlockSpec((1, 1, halo, kv_width), lambda b, r, n: (b, r, jnp.maximum(n * per - 1, 0), cb)),
            pl.BlockSpec((1, 1, tq, kv_width), lambda b, r, n: (b, r, n, cb)),
            pl.BlockSpec((1, 1, halo, kv_width), lambda b, r, n: (b, r, jnp.minimum((n + 1) * per, last), cb)),
        ]

    in_specs = [q_spec] + kv_specs(k_cb) + kv_specs(v_cb)
    args = [qkv] * 7
    if sink is not None:
        in_specs = [pl.BlockSpec(memory_space=pltpu.SMEM)] + in_specs
        args = [sink] + args
    o_spec = pl.BlockSpec((1, 1, tq, 512), lambda b, r, n: (b, r, n, 0))
    out_specs = [o_spec]
    out_shape = [jax.ShapeDtypeStruct((batch, nres, length, 512), out_dtype)]
    if has_lse:
        out_specs.append(o_spec)
        out_shape.append(jax.ShapeDtypeStruct((batch, nres, length, 512), F32))
    kern = functools.partial(_attn_kernel, tq=tq, halo=halo, sub=sub, length=length, kv_shared=kv_shared,
                             has_sink=sink is not None, has_lse=has_lse)
    return pl.pallas_call(
        kern,
        name=name,
        grid=(batch, nres, nb),
        in_specs=in_specs,
        out_specs=out_specs,
        out_shape=out_shape,
        scratch_shapes=[pltpu.VMEM((tq + 2 * halo, kv_width), BF16), pltpu.VMEM((tq + 2 * halo, kv_width), BF16)],
        compiler_params=_params("parallel", "parallel", "parallel"),
    )(*args)


def _mixer_a(groups):
    res = []
    for g, qkv in enumerate(groups):
        res.append(_banded_attention(qkv, name=f"attn_a{g}", q_cb=0, k_cb=1, v_cb=2, kv_width=512, halo=A_HALO,
                                     tq=min(512, qkv.shape[2]), sub=128, kv_shared=False, has_lse=True))
    return res


def _mixer_b(qkv_b, sink, batch, seq):
    (o,) = _banded_attention(
        qkv_b.reshape(batch, 1, seq, K1_B), name="attn_b", q_cb=0, k_cb=2, v_cb=3, kv_width=256,
        halo=B_HALF_WINDOW, tq=512, sub=128, kv_shared=True, sink=sink, out_dtype=BF16)
    return o.reshape(batch * seq, 512)


C_TILE = 512
SCAN_TILE = 256
GROUP = 256
N_CHUNK = C_TILE // C_CHUNK
SCAN_CHUNKS = SCAN_TILE // C_CHUNK


def _head_mask():
    r = lax.broadcasted_iota(jnp.int32, (GROUP, GROUP), 0) // HEAD_DIM
    c = lax.broadcasted_iota(jnp.int32, (GROUP, GROUP), 1) // HEAD_DIM
    return r == c


def _blockdiag(x, headmask):
    xb = x.astype(BF16)
    return jnp.concatenate([xb, xb, xb, xb], axis=0) * headmask


def _delta_chunks_kernel(qkv_ref, ba_ref, alog_ref, dtb_ref,
                         uf_ref, wf_ref, qkf_ref, qgf_ref, kdf_ref, glf_ref,
                         ub_ref, wb_ref, qkb_ref, qgb_ref, kdb_ref, glb_ref,
                         beta_scr, gi_scr):
    outs = ((uf_ref, wf_ref, qkf_ref, qgf_ref, kdf_ref, glf_ref), (ub_ref, wb_ref, qkb_ref, qgb_ref, kdb_ref, glb_ref))
    ones = _head_ones(GROUP)

    ba = ba_ref[0]
    beta_all = _sigmoid(ba)
    zz = ba + dtb_ref[...]
    g_all = -jnp.exp(alog_ref[...]) * (jnp.maximum(zz, 0.0) + jnp.log(1.0 + jnp.exp(-jnp.abs(zz))))
    lane_head = lax.broadcasted_iota(jnp.int32, (LANES, 512), 1) // HEAD_DIM
    krow = lax.broadcasted_iota(jnp.int32, (LANES, 512), 0)
    ti = lax.broadcasted_iota(jnp.int32, (MXU, MXU), 0)
    tj = lax.broadcasted_iota(jnp.int32, (MXU, MXU), 1)
    same_chunk = (ti // C_CHUNK) == (tj // C_CHUNK)
    for d in range(2):
        e_beta = jnp.where(krow == lane_head + d * C_HEADS, 1.0, 0.0).astype(BF16)
        e_g = jnp.where(krow == lane_head + (2 + d) * C_HEADS, 1.0, 0.0).astype(BF16)
        beta_scr[d] = _dot_unit_sel(beta_all, e_beta)
        tri = jnp.where(same_chunk & ((tj <= ti) if d == 0 else (tj >= ti)), 1.0, 0.0).astype(BF16)
        for r in range(0, C_TILE, MXU):
            gi_scr[d, r:r + MXU, :] = _dot_data_sel(_dot_sel_data(tri, g_all[r:r + MXU]), e_g)

    ii = lax.broadcasted_iota(jnp.int32, (C_CHUNK, 512), 0)
    jj = lax.broadcasted_iota(jnp.int32, (C_CHUNK, 512), 1) % C_CHUNK
    ig = lax.broadcasted_iota(jnp.int32, (C_CHUNK, GROUP), 0)
    jg = lax.broadcasted_iota(jnp.int32, (C_CHUNK, GROUP), 1) % C_CHUNK
    incl = (ii >= jj, ii <= jj)
    incl_g = (ig >= jg, ig <= jg)
    strict_g = (ig > jg, ig < jg)
    eye = jnp.where(ig == jg, 1.0, 0.0)
    diag16 = (ig // 16) == (jg // 16)
    headmask = ones
    zero_rows = jnp.zeros((LANES - C_CHUNK, LANES), F32)

    def bd(qs):
        return [_blockdiag(q, headmask) for q in qs]

    def mm(ps, bds):
        return [_dot(p.astype(BF16), b) for p, b in zip(ps, bds)]

    def mm2(ps, rs, bds):
        both = [_dot(jnp.concatenate([p, r], axis=0).astype(BF16), b) for p, r, b in zip(ps, rs, bds)]
        return [x[:C_CHUNK] for x in both], [x[C_CHUNK:] for x in both]

    def add(xs, ys):
        return [x + y for x, y in zip(xs, ys)]

    def unit_inverses(ls):
        dg = [jnp.where(diag16, l, 0.0) for l in ls]
        og = [l - d for l, d in zip(ls, dg)]
        di = [eye - d for d in dg]
        d2 = mm(dg, bd(dg))
        d4, t = mm2(d2, di, bd(d2))
        di = add(di, t)
        d8, t = mm2(d4, di, bd(d4))
        di = add(di, t)
        di = add(di, mm(di, bd(d8)))
        nn = mm(di, bd(og))
        m = [eye - x for x in nn]
        nbd = bd(nn)
        m = add(m, mm(mm(m, nbd), nbd))
        return mm(m, bd(di))

    def transposed_decay(gi_):
        parts = []
        for p in range(512 // LANES):
            t = jnp.concatenate([gi_[:, p * LANES:(p + 1) * LANES], zero_rows], axis=0).T
            parts.append(t[:C_CHUNK, :] + pltpu.roll(t[C_CHUNK:, :], HEAD_DIM, 1))
        return jnp.concatenate(parts, axis=1)

    def chunks(cs):
        lms, vbs, kbgs, dest = [], [], [], []
        for c in cs:
            rows = pl.ds(pl.multiple_of(c * C_CHUNK, C_CHUNK), C_CHUNK)
            qc, kc, vc = (qkv_ref[0, rows, i * C_WIDTH:(i + 1) * C_WIDTH] for i in range(3))
            kk, qk = [], []
            for gidx in range(2):
                sl = slice(gidx * GROUP, (gidx + 1) * GROUP)
                kbd = _blockdiag(kc[:, sl], headmask)
                kk.append(_dot_nt(kc[:, sl].astype(BF16), kbd))
                qk.append(_dot_nt(qc[:, sl].astype(BF16), kbd))
            for d in range(2):
                u_ref, w_ref, qk_ref, qg_ref, kd_ref, gl_ref = outs[d]
                beta = beta_scr[d, rows, :]
                gi_ = gi_scr[d, rows, :]
                decay = jnp.exp(jnp.where(incl[d], gi_ - transposed_decay(gi_), NEG_INF))
                eg = jnp.exp(gi_)
                last = C_CHUNK - 1 if d == 0 else 0
                glrow = gi_[last:last + 1, :]
                kd_ref[0, rows, :] = (kc * jnp.exp(glrow - gi_)).astype(BF16)
                qg_ref[0, rows, :] = (qc * eg).astype(BF16)
                gl_ref[0, pl.ds(c, 1), 0, :] = jnp.exp(glrow)
                kbg = kc * beta * eg
                vb = vc * beta
                for gidx in range(2):
                    sl = slice(gidx * GROUP, (gidx + 1) * GROUP)
                    lms.append(jnp.where(strict_g[d], kk[gidx] * beta[:, sl] * decay[:, sl], 0.0))
                    qk_ref[0, rows, sl] = jnp.where(incl_g[d], qk[gidx] * decay[:, sl], 0.0).astype(BF16)
                    vbs.append(vb[:, sl])
                    kbgs.append(kbg[:, sl])
                    dest.append((u_ref, w_ref, rows, sl))
        tinvs = unit_inverses(lms)
        uw = mm(tinvs, [jnp.concatenate([a, b], axis=1) for a, b in zip(bd(vbs), bd(kbgs))])
        for (u_ref, w_ref, rows, sl), x in zip(dest, uw):
            u_ref[0, rows, sl] = x[:, :GROUP]
            w_ref[0, rows, sl] = x[:, GROUP:].astype(BF16)

    def body(c2, carry):
        chunks((2 * c2, 2 * c2 + 1))
        return carry

    lax.fori_loop(0, N_CHUNK // 2, body, 0)


def _delta_chunks(qkv, ba, alog_row, dtb_row):
    batch, seq, _ = qkv.shape
    nt = seq // C_TILE
    tile = lambda b, n: (b, n, 0)
    wide = lambda dt: jax.ShapeDtypeStruct((batch, seq, C_WIDTH), dt)
    gl_sds = jax.ShapeDtypeStruct((batch, seq // C_CHUNK, 1, C_WIDTH), F32)
    out_shape = [wide(F32), wide(BF16), wide(BF16), wide(BF16), wide(BF16), gl_sds] * 2
    wide_spec = pl.BlockSpec((1, C_TILE, C_WIDTH), tile)
    gl_spec = pl.BlockSpec((1, N_CHUNK, 1, C_WIDTH), lambda b, n: (b, n, 0, 0))
    return pl.pallas_call(
        _delta_chunks_kernel,
        name="delta_chunks",
        grid=(batch, nt),
        in_specs=[
            pl.BlockSpec((1, C_TILE, K1_C), tile),
            pl.BlockSpec((1, C_TILE, K1_BA), tile),
            _const_spec((1, LANES)), _const_spec((1, LANES))],
        out_specs=[wide_spec] * 5 + [gl_spec] + [wide_spec] * 5 + [gl_spec],
        out_shape=out_shape,
        scratch_shapes=[
            pltpu.VMEM((2, C_TILE, C_WIDTH), F32),
            pltpu.VMEM((2, C_TILE, C_WIDTH), F32),
        ],
        compiler_params=_params("parallel", "parallel"),
    )(qkv, ba, alog_row, dtb_row)


def _delta_scan_kernel(uf_ref, wf_ref, qkf_ref, qgf_ref, kdf_ref, glf_ref,
                       ub_ref, wb_ref, qkb_ref, qgb_ref, kdb_ref, glb_ref, of_ref, ob_ref, state_scr):
    n = pl.program_id(1)
    bblk = uf_ref.shape[0]
    ins = ((uf_ref, wf_ref, qkf_ref, qgf_ref, kdf_ref, glf_ref, of_ref),
           (ub_ref, wb_ref, qkb_ref, qgb_ref, kdb_ref, glb_ref, ob_ref))

    @pl.when(n == 0)
    def _():
        state_scr[...] = jnp.zeros_like(state_scr)

    headmask = _head_mask()
    head01 = _head_ones(GROUP)

    def body(c, carry):
        chains = []
        for d in range(2):
            cc = c if d == 0 else SCAN_CHUNKS - 1 - c
            rows = pl.ds(pl.multiple_of(cc * C_CHUNK, C_CHUNK), C_CHUNK)
            for b in range(bblk):
                for gidx in range(2):
                    chains.append((d, b, gidx, cc, rows, slice(gidx * GROUP, (gidx + 1) * GROUP)))
        states = [state_scr[b, d, gidx] for d, b, gidx, _, _, _ in chains]
        both = [_dot(jnp.concatenate([ins[d][1][b, rows, sl], ins[d][3][b, rows, sl]], axis=0), st.astype(BF16))
                for (d, b, _, _, rows, sl), st in zip(chains, states)]
        v_new = [ins[d][0][b, rows, sl] - bo[:C_CHUNK] for (d, b, _, _, rows, sl), bo in zip(chains, both)]
        intra = [_dot(ins[d][2][b, rows, sl], _blockdiag(v, head01)) for (d, b, _, _, rows, sl), v in zip(chains, v_new)]
        upd = [_dot_tn(ins[d][4][b, rows, sl], v.astype(BF16)) for (d, b, _, _, rows, sl), v in zip(chains, v_new)]
        for (d, b, gidx, cc, rows, sl), st, bo, it, up in zip(chains, states, both, intra, upd):
            ins[d][6][b, rows, sl] = bo[C_CHUNK:] + it
            gl = ins[d][5][b, pl.ds(cc, 1), 0, :]
            state_scr[b, d, gidx] = st * gl[:, sl] + jnp.where(headmask, up, 0.0)
        return carry

    lax.fori_loop(0, SCAN_CHUNKS, body, 0)


def _delta_scan(factors):
    batch, seq, _ = factors[0].shape
    nt = seq // SCAN_TILE
    bblk = min(batch, 4)
    fwd = lambda b, n: (b, n, 0)
    bwd = lambda b, n: (b, nt - 1 - n, 0)

    def specs(tile):
        wide = pl.BlockSpec((bblk, SCAN_TILE, C_WIDTH), tile)
        gl = pl.BlockSpec((bblk, SCAN_CHUNKS, 1, C_WIDTH), lambda b, n: tile(b, n) + (0,))
        return [wide] * 5 + [gl]

    out_sds = jax.ShapeDtypeStruct((batch, seq, C_WIDTH), F32)
    return pl.pallas_call(
        _delta_scan_kernel,
        name="delta_scan",
        grid=(batch // bblk, nt),
        in_specs=specs(fwd) + specs(bwd),
        out_specs=[pl.BlockSpec((bblk, SCAN_TILE, C_WIDTH), fwd), pl.BlockSpec((bblk, SCAN_TILE, C_WIDTH), bwd)],
        out_shape=[out_sds, out_sds],
        scratch_shapes=[pltpu.VMEM((bblk, 2, 2, GROUP, GROUP), F32)],
        compiler_params=_params("parallel", "arbitrary"),
    )(*factors)


def _merge_kernel(x_ref, n1_ref, oa0, la0, oa1, la1, oa2, la2, ob_ref, ocf_ref, ocb_ref, z_ref, ogain_ref,
                  wg_ref, wba_ref, wbb_ref, wbc_ref, wo_ref, out_ref, inter_scr):
    tm = x_ref.shape[0]
    x = x_ref[...]
    h = _rms_rows(x, n1_ref[...]).astype(BF16)

    d1, d2 = A_DILATIONS[1], A_DILATIONS[2]
    ones = _head_ones(MXU)
    n_col = D_MODEL // MXU

    def interleaved(ref, d, slot, lanes):
        for r in range(d):
            inter_scr[slot, pl.ds(r, tm // d, stride=d), :] = ref[0, r, :, lanes]
        return inter_scr[slot]

    def mix_a(c):
        lanes = slice(c * LANES, (c + 1) * LANES)
        l0, o0 = la0[:, lanes], oa0[:, lanes]
        l1, o1 = interleaved(la1, d1, c, lanes), interleaved(oa1, d1, 4 + c, lanes)
        l2, o2 = interleaved(la2, d2, 8 + c, lanes), interleaved(oa2, d2, 12 + c, lanes)
        m = jnp.maximum(jnp.maximum(l0, l1), l2)
        e0, e1, e2 = jnp.exp2(l0 - m), jnp.exp2(l1 - m), jnp.exp2(l2 - m)
        return ((e0 * o0 + e1 * o1 + e2 * o2) / (e0 + e1 + e2)).astype(BF16)

    def norm_c(c):
        lanes = slice(c * MXU, (c + 1) * MXU)
        blk = ocf_ref[:, lanes] + ocb_ref[:, lanes]
        ss = _dot((blk * blk).astype(BF16), ones)
        zz = z_ref[:, lanes]
        return (blk * lax.rsqrt(ss * (1.0 / HEAD_DIM) + EPS) * ogain_ref[:, lanes] * (zz * _sigmoid(zz))).astype(BF16)

    def term(br, o_br, w_ref, c):
        gate = _sigmoid(_dot(h, wg_ref[:, br * D_MODEL + c * MXU:br * D_MODEL + (c + 1) * MXU]))
        return gate * _dot(o_br, w_ref[:, c * MXU:(c + 1) * MXU])

    o_b = ob_ref[...]
    term_b, oa_tiles = [], []
    for c in range(n_col):
        term_b.append(term(1, o_b, wbb_ref, c))
        oa_tiles.append(mix_a(c))
    o_a = jnp.concatenate(oa_tiles, axis=1)
    merged, oc_tiles = [], []
    for c in range(n_col):
        merged.append(term(0, o_a, wba_ref, c) + term_b[c])
        if c < C_WIDTH // MXU:
            oc_tiles.append(norm_c(c))
    o_c = jnp.concatenate(oc_tiles, axis=1)
    merged = [m + term(2, o_c, wbc_ref, c) for c, m in enumerate(merged)]
    out_ref[...] = x + _dot(jnp.concatenate(merged, axis=1).astype(BF16), wo_ref[...])


def _merge(x2, n1, a_res, o_b, o_cf, o_cb, z, ogain, wg, wba, wbb, wbc, wo, seq):
    tm = TOKEN_TILE
    t = x2.shape[0]
    ns = seq // tm
    row = lambda i: (i, 0)
    res = lambda i: (i // ns, 0, i % ns, 0)
    wide = pl.BlockSpec((tm, D_MODEL), row)
    half = pl.BlockSpec((tm, 512), row)
    d1, d2 = A_DILATIONS[1], A_DILATIONS[2]
    res1 = pl.BlockSpec((1, d1, tm // d1, 512), res)
    res2 = pl.BlockSpec((1, d2, tm // d2, 512), res)
    (o0, l0), (o1, l1), (o2, l2) = a_res
    return pl.pallas_call(
        _merge_kernel,
        name="merge",
        grid=(t // tm,),
        in_specs=[wide, _const_spec((1, D_MODEL)), half, half, res1, res1, res2, res2] + [half] * 4 + [
            _const_spec((1, C_WIDTH)), _const_spec((D_MODEL, N_BRANCH * D_MODEL)), _const_spec((512, D_MODEL)),
            _const_spec((512, D_MODEL)), _const_spec((512, D_MODEL)), _const_spec((D_MODEL, D_MODEL))],
        out_specs=wide,
        out_shape=jax.ShapeDtypeStruct((t, D_MODEL), F32),
        scratch_shapes=[pltpu.VMEM((16, tm, LANES), F32)],
        compiler_params=_params("parallel"),
    )(x2, n1, o0.reshape(t, 512), l0.reshape(t, 512), o1, l1, o2, l2, o_b, o_cf, o_cb, z, ogain, wg, wba, wbb, wbc, wo)


def _ffn_kernel(x_ref, n2_ref, wi_ref, wo_ref, out_ref):
    x = x_ref[...]
    h = _rms_rows(x, n2_ref[...]).astype(BF16)
    acc = x
    for c in range(0, FFN_HIDDEN, MXU):
        gate = _dot(h, wi_ref[:, c:c + MXU])
        up = _dot(h, wi_ref[:, FFN_HIDDEN + c:FFN_HIDDEN + c + MXU])
        act = (gate * _sigmoid(gate) * up).astype(BF16)
        acc = acc + _dot(act, wo_ref[c:c + MXU, :])
    out_ref[...] = acc


def _ffn(x2, n2, wi, wo):
    tm = TOKEN_TILE
    t = x2.shape[0]
    row = lambda i: (i, 0)
    return pl.pallas_call(
        _ffn_kernel,
        name="ffn",
        grid=(t // tm,),
        in_specs=[pl.BlockSpec((tm, D_MODEL), row), _const_spec((1, D_MODEL)),
                  _const_spec((D_MODEL, 2 * FFN_HIDDEN)), _const_spec((FFN_HIDDEN, D_MODEL))],
        out_specs=pl.BlockSpec((tm, D_MODEL), row),
        out_shape=jax.ShapeDtypeStruct((t, D_MODEL), F32),
        compiler_params=_params("parallel"),
    )(x2, n2, wi, wo)


def _rope_tables(seq):
    pos = jnp.arange(seq, dtype=F32)
    inv_freq = jnp.power(jnp.float32(ROPE_THETA), -jnp.arange(0, ROPE_DIM, 2, dtype=F32) / ROPE_DIM)
    ang = pos[:, None] * inv_freq[None, :]
    cos, sin = jnp.cos(ang), jnp.sin(ang)
    half = ROPE_DIM // 2
    one = jnp.ones((seq, HEAD_DIM - ROPE_DIM), F32)
    zero = jnp.zeros((seq, HEAD_DIM - half), F32)
    cos_h = jnp.concatenate([cos, cos, one], axis=1)
    sa_h = jnp.concatenate([-sin, zero], axis=1)
    sb_h = jnp.concatenate([jnp.zeros((seq, half), F32), sin, zero[:, half:]], axis=1)
    tile = lambda t: jnp.concatenate([t, t], axis=1)
    return tile(cos_h), tile(sa_h), tile(sb_h)


def _pack_layer(l, norm1, w_in, qk_gain, sink, conv_w, a_log, dt_bias, o_gain, w_gate, w_br_a, w_br_b, w_br_c,
                w_out, norm2, w_ffn_in, w_ffn_out):
    wi = w_in[l]
    kb0, kb1 = wi[:, 5120:5184], wi[:, 5184:5248]
    vb0, vb1 = wi[:, 5248:5312], wi[:, 5312:5376]
    w1 = jnp.concatenate([
        wi[:, :5120], kb0, kb0, kb1, kb1, vb0, vb0, vb1, vb1, wi[:, 5376:7424],
        wi[:, 7424:7456], jnp.zeros((D_MODEL, K1_BA - 32), F32)], axis=1).astype(BF16)
    scale = HEAD_DIM ** -0.5 * LOG2_E
    gains = jnp.stack([qk_gain[l, 0] * scale, qk_gain[l, 1], qk_gain[l, 2] * scale, qk_gain[l, 3]])
    gains = jnp.concatenate([gains, gains], axis=1)
    pad = lambda v: jnp.concatenate([jnp.zeros((16,), F32), v.reshape(-1), jnp.zeros((LANES - 32,), F32)])[None, :]
    return dict(
        n1=norm1[l][None, :], w1=w1, gains=gains, sink=sink[l] * LOG2_E, conv_w=conv_w[l],
        alog=pad(a_log[l]), dtb=pad(dt_bias[l]), ogain=jnp.tile(o_gain[l], C_HEADS)[None, :],
        wg=w_gate[l].astype(BF16), wba=w_br_a[l].astype(BF16), wbb=w_br_b[l].astype(BF16),
        wbc=w_br_c[l].astype(BF16), wo=w_out[l].astype(BF16), n2=norm2[l][None, :],
        wfi=w_ffn_in[l].astype(BF16), wfo=w_ffn_out[l].astype(BF16))


def _layer(x2, p, tabs, batch, seq):
    t = batch * seq
    a0, a1, a2, qkv_b, qkv_c, z, ba = _inproj(x2, p["n1"], p["w1"], *tabs, p["gains"], p["conv_w"], batch, seq)
    a_res = _mixer_a((a0.reshape(batch, 1, seq, A_QKV), a1, a2))
    o_b = _mixer_b(qkv_b, p["sink"], batch, seq)
    factors = _delta_chunks(qkv_c.reshape(batch, seq, K1_C), ba.reshape(batch, seq, K1_BA), p["alog"], p["dtb"])
    o_cf, o_cb = _delta_scan(factors)
    x2 = _merge(x2, p["n1"], a_res, o_b, o_cf.reshape(t, C_WIDTH), o_cb.reshape(t, C_WIDTH), z, p["ogain"],
                p["wg"], p["wba"], p["wbb"], p["wbc"], p["wo"], seq)
    return _ffn(x2, p["n2"], p["wfi"], p["wfo"])


def kernel(x_prompt, x_sample, norm1, w_in, qk_gain, sink, conv_w, a_log, dt_bias, o_gain, w_gate, w_br_a, w_br_b,
           w_br_c, w_out, norm2, w_ffn_in, w_ffn_out):
    depth = w_in.shape[0]
    layers = [_pack_layer(l, norm1, w_in, qk_gain, sink, conv_w, a_log, dt_bias, o_gain, w_gate, w_br_a, w_br_b,
                          w_br_c, w_out, norm2, w_ffn_in, w_ffn_out) for l in range(depth)]
    outs = []
    for x in (x_prompt, x_sample):
        batch, seq, _ = x.shape
        assert seq % (A_DILATIONS[-1] * 256) == 0
        tabs = _rope_tables(seq)
        x2 = x.reshape(batch * seq, D_MODEL)
        for p in layers:
            x2 = _layer(x2, p, tabs, batch, seq)
        outs.append(x2.reshape(batch, seq, D_MODEL))
    return tuple(outs)
```

```python
import functools

import jax
import jax.numpy as jnp
from jax import lax
from jax.experimental import pallas as pl
from jax.experimental.pallas import tpu as pltpu

F32 = jnp.float32
BF16 = jnp.bfloat16

D_MODEL = 1024
HEAD_DIM = 64
ROPE_DIM = HEAD_DIM // 4
ROPE_THETA = 500000.0
EPS = 1e-6
NEG_INF = -1e30
LOG2_E = 1.4426950408889634

A_PATTERNS = ((128, 1), (512, 4), (2048, 16))
A_DILATIONS = tuple(d for _, d in A_PATTERNS)
A_GROUPS = 3
A_WIDTH = 512
A_QKV = A_GROUPS * A_WIDTH
A_HALO = 64
B_HALF_WINDOW = 128
C_HEADS = 8
C_CHUNK = 64
C_CONV = 5
C_WIDTH = 512
FFN_HIDDEN = 2816
N_BRANCH = 3

LANES = 128
MXU = 256
VMEM_LIMIT = 56 * 1024 * 1024
TOKEN_TILE = 512
IN_HALO = 16

K1_A = 3 * A_QKV
K1_B = 512 + 256 + 256
K1_C = 3 * C_WIDTH
K1_Z = C_WIDTH
K1_BA = LANES
K1_OFF_B = K1_A
K1_OFF_C = K1_OFF_B + K1_B
K1_OFF_Z = K1_OFF_C + K1_C
K1_OFF_BA = K1_OFF_Z + K1_Z
K1_WIDTH = K1_OFF_BA + K1_BA


def _dot(a, b):
    return jnp.dot(a, b, preferred_element_type=F32)


def _dot_nt(a, b):
    return lax.dot_general(a, b, (((1,), (1,)), ((), ())), preferred_element_type=F32)


def _dot_tn(a, b):
    return lax.dot_general(a, b, (((0,), (0,)), ((), ())), preferred_element_type=F32)


def _split3(a):
    a1 = a.astype(BF16)
    r1 = a - a1.astype(F32)
    a2 = r1.astype(BF16)
    a3 = (r1 - a2.astype(F32)).astype(BF16)
    return a1, a2, a3


def _dot_data_sel(a, sel):
    a1, a2, a3 = _split3(a)
    return _dot(a1, sel) + _dot(a2, sel) + _dot(a3, sel)


def _dot_unit_sel(a, sel):
    a1, a2, _ = _split3(a)
    return _dot(a1, sel) + _dot(a2, sel)


def _dot_sel_data(sel, b):
    b1, b2, b3 = _split3(b)
    return _dot(sel, b1) + _dot(sel, b2) + _dot(sel, b3)


def _rms_rows(x, gain):
    ms = jnp.mean(x * x, axis=-1, keepdims=True)
    return x * lax.rsqrt(ms + EPS) * gain


def _sigmoid(x):
    return 0.5 * jnp.tanh(0.5 * x) + 0.5


def _head_ones(n):
    r = lax.broadcasted_iota(jnp.int32, (n, n), 0) // HEAD_DIM
    c = lax.broadcasted_iota(jnp.int32, (n, n), 1) // HEAD_DIM
    return jnp.where(r == c, 1.0, 0.0).astype(BF16)


def _const_spec(shape):
    nd = len(shape)
    return pl.BlockSpec(shape, lambda *_: (0,) * nd, pipeline_mode=pl.Buffered(1))


def _params(*sem):
    return pltpu.CompilerParams(dimension_semantics=sem, vmem_limit_bytes=VMEM_LIMIT)


def _inproj_kernel(x_ref, xp_ref, xn_ref, n1_ref, w_ref, cos_ref, sa_ref, sb_ref, gain_ref, convw_ref,
                   a0_ref, a1_ref, a2_ref, ob_ref, oc_ref, oz_ref, oba_ref, deint_scr, conv_scr, *, tiles_per_seq):
    tm = x_ref.shape[0]
    x_ext = jnp.concatenate([xp_ref[...], x_ref[...], xn_ref[...]], axis=0)
    h_ext = _rms_rows(x_ext, n1_ref[...]).astype(BF16)
    h = h_ext[IN_HALO:IN_HALO + tm]
    ones = _head_ones(MXU)
    tile_in_seq = pl.program_id(0) % tiles_per_seq
    ext_row = lax.broadcasted_iota(jnp.int32, (tm + 2 * IN_HALO, 1), 0)
    in_seq = (((ext_row >= IN_HALO) | (tile_in_seq > 0))
              & ((ext_row < IN_HALO + tm) | (tile_in_seq < tiles_per_seq - 1)))
    cos, sa, sb = cos_ref[...], sa_ref[...], sb_ref[...]
    a_refs = (a0_ref, a1_ref, a2_ref)
    slot = [0]

    def qk_norm_rope(acc, gain):
        ss = _dot((acc * acc).astype(BF16), ones)
        r = lax.rsqrt(ss * (1.0 / HEAD_DIM) + EPS)
        outs = []
        for s in range(0, MXU, LANES):
            y = acc[:, s:s + LANES] * r[:, s:s + LANES] * gain
            outs.append(y * cos + pltpu.roll(y, LANES - 8, 1) * sa + pltpu.roll(y, 8, 1) * sb)
        return outs

    def emit_a(g, col, val):
        d = A_DILATIONS[g]
        if d == 1:
            a0_ref[:, col:col + LANES] = val.astype(BF16)
            return
        s = slot[0]
        slot[0] = (s + 1) % deint_scr.shape[0]
        deint_scr[s] = val
        for r in range(d):
            a_refs[g][0, r, :, col:col + LANES] = deint_scr[s, pl.ds(r, tm // d, stride=d), :].astype(BF16)

    items = []

    def a_item(kind, g, c):
        def finish(acc):
            parts = qk_norm_rope(acc, gain_ref[kind:kind + 1, :]) if kind < 2 else [acc[:, :LANES], acc[:, LANES:]]
            for i, part in enumerate(parts):
                emit_a(g, kind * A_WIDTH + c + i * LANES, part)
        items.append((kind * A_QKV + g * A_WIDTH + c, MXU, finish))

    def qk_item(c0, gain_row, out_ref, o0):
        def finish(acc):
            for i, part in enumerate(qk_norm_rope(acc, gain_ref[gain_row:gain_row + 1, :])):
                out_ref[:, o0 + i * LANES:o0 + (i + 1) * LANES] = part.astype(out_ref.dtype)
        items.append((c0, MXU, finish))

    def plain_item(c0, width, out_ref, o0):
        def finish(acc):
            out_ref[:, o0:o0 + width] = acc.astype(out_ref.dtype)
        items.append((c0, width, finish))

    for kind in range(3):
        for g in range(A_GROUPS):
            for c in range(0, A_WIDTH, MXU):
                a_item(kind, g, c)
    for c in range(0, 512, MXU):
        qk_item(K1_OFF_B + c, 2, ob_ref, c)
    qk_item(K1_OFF_B + 512, 3, ob_ref, 512)
    plain_item(K1_OFF_B + 768, 256, ob_ref, 768)
    def conv_item(j):
        c = j * MXU

        def finish(acc):
            buf = j % conv_scr.shape[0]
            conv_scr[buf] = jnp.where(in_seq, acc, 0.0)
            y = None
            for tap in range(C_CONV):
                off = IN_HALO - C_CONV // 2 + tap
                term = conv_scr[buf, off:off + tm, :] * convw_ref[tap:tap + 1, c:c + MXU]
                y = term if y is None else y + term
            y = y * _sigmoid(y)
            if c < 2 * C_WIDTH:
                ss = _dot((y * y).astype(BF16), ones)
                y = y * lax.rsqrt(ss + EPS) * (HEAD_DIM ** -0.5 if c < C_WIDTH else 1.0)
            oc_ref[:, c:c + MXU] = y
        items.append((K1_OFF_C + c, MXU, finish, h_ext))

    for c in range(0, K1_Z, MXU):
        plain_item(K1_OFF_Z + c, MXU, oz_ref, c)
    plain_item(K1_OFF_BA, K1_BA, oba_ref, 0)
    n_conv = K1_C // MXU
    gap = len(items) // n_conv
    for j in range(n_conv):
        conv_item(j)
        items.insert(j * (gap + 1) + gap // 2, items.pop())

    product = lambda it: _dot(it[3] if len(it) > 3 else h, w_ref[:, it[0]:it[0] + it[1]])
    pending, issued = {}, 0
    for i, it in enumerate(items):
        while issued <= min(i + (2 if len(it) > 3 else 1), len(items) - 1):
            pending[issued] = product(items[issued])
            issued += 1
        it[2](pending.pop(i))


def _inproj(x2, n1, w, cos_t, sa_t, sb_t, gains, conv_w, batch, seq):
    tm = TOKEN_TILE
    t = x2.shape[0]
    ns = seq // tm
    row = lambda i: (i, 0)
    pos = lambda i: (i % ns, 0)
    d1, d2 = A_DILATIONS[1], A_DILATIONS[2]
    res = lambda i: (i // ns, 0, i % ns, 0)
    per = tm // IN_HALO
    last = t // IN_HALO - 1
    return pl.pallas_call(
        functools.partial(_inproj_kernel, tiles_per_seq=ns),
        name="inproj",
        grid=(t // tm,),
        in_specs=[
            pl.BlockSpec((tm, D_MODEL), row),
            pl.BlockSpec((IN_HALO, D_MODEL), lambda i: (jnp.maximum(i * per - 1, 0), 0)),
            pl.BlockSpec((IN_HALO, D_MODEL), lambda i: (jnp.minimum((i + 1) * per, last), 0)),
            _const_spec((1, D_MODEL)),
            _const_spec((D_MODEL, K1_WIDTH)),
            pl.BlockSpec((tm, LANES), pos),
            pl.BlockSpec((tm, LANES), pos),
            pl.BlockSpec((tm, LANES), pos),
            _const_spec((4, LANES)),
            _const_spec((C_CONV, K1_C)),
        ],
        out_specs=[
            pl.BlockSpec((tm, A_QKV), row),
            pl.BlockSpec((1, d1, tm // d1, A_QKV), res),
            pl.BlockSpec((1, d2, tm // d2, A_QKV), res),
            pl.BlockSpec((tm, K1_B), row),
            pl.BlockSpec((tm, K1_C), row),
            pl.BlockSpec((tm, K1_Z), row),
            pl.BlockSpec((tm, K1_BA), row),
        ],
        out_shape=[
            jax.ShapeDtypeStruct((t, A_QKV), BF16),
            jax.ShapeDtypeStruct((batch, d1, seq // d1, A_QKV), BF16),
            jax.ShapeDtypeStruct((batch, d2, seq // d2, A_QKV), BF16),
            jax.ShapeDtypeStruct((t, K1_B), BF16),
            jax.ShapeDtypeStruct((t, K1_C), F32),
            jax.ShapeDtypeStruct((t, K1_Z), F32),
            jax.ShapeDtypeStruct((t, K1_BA), F32),
        ],
        scratch_shapes=[pltpu.VMEM((4, tm, LANES), F32), pltpu.VMEM((2, tm + 2 * IN_HALO, MXU), F32)],
        compiler_params=_params("parallel"),
    )(x2, x2, x2, n1, w, cos_t, sa_t, sb_t, gains, conv_w)


def _attn_kernel(*refs, tq, halo, sub, length, kv_shared, has_sink, has_lse):
    refs = list(refs)
    sink_ref = refs.pop(0) if has_sink else None
    q_ref, kp_ref, kc_ref, kn_ref, vp_ref, vc_ref, vn_ref, o_ref = refs[:8]
    rest = refs[8:]
    lse_ref = rest.pop(0) if has_lse else None
    kbuf, vbuf = rest
    n = pl.program_id(2)
    nk = sub + 2 * halo

    kbuf[0:halo, :] = kp_ref[0, 0]
    kbuf[halo:halo + tq, :] = kc_ref[0, 0]
    kbuf[halo + tq:, :] = kn_ref[0, 0]
    vbuf[0:halo, :] = vp_ref[0, 0]
    vbuf[halo:halo + tq, :] = vc_ref[0, 0]
    vbuf[halo + tq:, :] = vn_ref[0, 0]

    rows = lax.broadcasted_iota(jnp.int32, (2 * sub, nk), 0) % sub
    cols = lax.broadcasted_iota(jnp.int32, (2 * sub, nk), 1)
    rel = cols - halo - rows
    band = (rel <= halo) & (rel >= -halo)
    lane_low = lax.broadcasted_iota(jnp.int32, (2 * sub, LANES), 1) < HEAD_DIM
    row_first = lax.broadcasted_iota(jnp.int32, (2 * sub, LANES), 0) < sub
    own_lanes = lane_low == row_first
    low_lane = lax.broadcasted_iota(jnp.int32, (sub, LANES), 1) < HEAD_DIM
    first_row = lax.broadcasted_iota(jnp.int32, (2 * sub, 1), 0) < sub

    for s in range(tq // sub):
        kpos = n * tq + (s * sub - halo) + cols
        mask = band & (kpos >= 0) & (kpos < length)
        qrows = slice(s * sub, (s + 1) * sub)
        krows = slice(s * sub, s * sub + nk)
        kcol = [(hp // 2 if kv_shared else hp) * LANES for hp in range(4)]
        sc = []
        for hp in range(4):
            qp = q_ref[0, 0, qrows, hp * LANES:(hp + 1) * LANES]
            qm = jnp.where(own_lanes, jnp.concatenate([qp, qp], axis=0), jnp.zeros((), qp.dtype))
            sc.append(jnp.where(mask, _dot_nt(qm, kbuf[krows, kcol[hp]:kcol[hp] + LANES]), NEG_INF))
        m = [jnp.max(x, axis=-1, keepdims=True) for x in sc]
        if has_sink:
            snk = [jnp.where(first_row, sink_ref[2 * hp], sink_ref[2 * hp + 1]) for hp in range(4)]
            m = [jnp.maximum(x, y) for x, y in zip(m, snk)]
        p = [jnp.exp2(x - y) for x, y in zip(sc, m)]
        den = [jnp.sum(x, axis=-1, keepdims=True) for x in p]
        if has_sink:
            den = [x + jnp.exp2(y - z) for x, y, z in zip(den, snk, m)]
        o = [_dot(x.astype(BF16), vbuf[krows, kc0:kc0 + LANES]) for x, kc0 in zip(p, kcol)]
        for hp in range(4):
            lanes = slice(hp * LANES, (hp + 1) * LANES)
            den_p = jnp.where(low_lane, den[hp][:sub], den[hp][sub:])
            o_p = jnp.where(low_lane, o[hp][:sub], o[hp][sub:]) / den_p
            o_ref[0, 0, qrows, lanes] = o_p.astype(o_ref.dtype)
            if has_lse:
                m_p = jnp.where(low_lane, m[hp][:sub], m[hp][sub:])
                lse_ref[0, 0, qrows, lanes] = m_p + jnp.log2(den_p)


def _banded_attention(qkv, *, name, q_cb, k_cb, v_cb, kv_width, halo, tq, sub, kv_shared, sink=None,
                      has_lse=False, out_dtype=F32):
    batch, nres, length, _ = qkv.shape
    nb = length // tq
    per = tq // halo
    last = length // halo - 1
    q_spec = pl.BlockSpec((1, 1, tq, 512), lambda b, r, n: (b, r, n, q_cb))

    def kv_specs(cb):
        return [
            pl.BlockSpec((1, 1, halo, kv_width), lambda b, r, n: (b, r, jnp.maximum(n * per - 1, 0), cb)),
            pl.BlockSpec((1, 1, tq, kv_width), lambda b, r, n: (b, r, n, cb)),
            pl.BlockSpec((1, 1, halo, kv_width), lambda b, r, n: (b, r, jnp.minimum((n + 1) * per, last), cb)),
        ]

    in_specs = [q_spec] + kv_specs(k_cb) + kv_specs(v_cb)
    args = [qkv] * 7
    if sink is not None:
        in_specs = [pl.BlockSpec(memory_space=pltpu.SMEM)] + in_specs
        args = [sink] + args
    o_spec = pl.BlockSpec((1, 1, tq, 512), lambda b, r, n: (b, r, n, 0))
    out_specs = [o_spec]
    out_shape = [jax.ShapeDtypeStruct((batch, nres, length, 512), out_dtype)]
    if has_lse:
        out_specs.append(o_spec)
        out_shape.append(jax.ShapeDtypeStruct((batch, nres, length, 512), F32))
    kern = functools.partial(_attn_kernel, tq=tq, halo=halo, sub=sub, length=length, kv_shared=kv_shared,
                             has_sink=sink is not None, has_lse=has_lse)
    return pl.pallas_call(
        kern,
        name=name,
        grid=(batch, nres, nb),
        in_specs=in_specs,
        out_specs=out_specs,
        out_shape=out_shape,
        scratch_shapes=[pltpu.VMEM((tq + 2 * halo, kv_width), BF16), pltpu.VMEM((tq + 2 * halo, kv_width), BF16)],
        compiler_params=_params("parallel", "parallel", "parallel"),
    )(*args)


def _mixer_a(groups):
    res = []
    for g, qkv in enumerate(groups):
        res.append(_banded_attention(qkv, name=f"attn_a{g}", q_cb=0, k_cb=1, v_cb=2, kv_width=512, halo=A_HALO,
                                     tq=min(1024, qkv.shape[2]), sub=128, kv_shared=False, has_lse=True))
    return res


def _mixer_b(qkv_b, sink, batch, seq):
    (o,) = _banded_attention(
        qkv_b.reshape(batch, 1, seq, K1_B), name="attn_b", q_cb=0, k_cb=2, v_cb=3, kv_width=256,
        halo=B_HALF_WINDOW, tq=1024, sub=128, kv_shared=True, sink=sink, out_dtype=BF16)
    return o.reshape(batch * seq, 512)


C_TILE = 512
SCAN_TILE = 256
GROUP = 256
N_CHUNK = C_TILE // C_CHUNK
SCAN_CHUNKS = SCAN_TILE // C_CHUNK


def _head_mask():
    r = lax.broadcasted_iota(jnp.int32, (GROUP, GROUP), 0) // HEAD_DIM
    c = lax.broadcasted_iota(jnp.int32, (GROUP, GROUP), 1) // HEAD_DIM
    return r == c


def _blockdiag(x, headmask):
    xb = x.astype(BF16)
    return jnp.concatenate([xb, xb, xb, xb], axis=0) * headmask


def _delta_chunks_kernel(qkv_ref, ba_ref, alog_ref, dtb_ref,
                         uf_ref, wf_ref, qkf_ref, qgf_ref, kdf_ref, glf_ref,
                         ub_ref, wb_ref, qkb_ref, qgb_ref, kdb_ref, glb_ref,
                         beta_scr, gi_scr):
    outs = ((uf_ref, wf_ref, qkf_ref, qgf_ref, kdf_ref, glf_ref), (ub_ref, wb_ref, qkb_ref, qgb_ref, kdb_ref, glb_ref))
    ones = _head_ones(GROUP)

    ba = ba_ref[0]
    beta_all = _sigmoid(ba)
    zz = ba + dtb_ref[...]
    g_all = -jnp.exp(alog_ref[...]) * (jnp.maximum(zz, 0.0) + jnp.log(1.0 + jnp.exp(-jnp.abs(zz))))
    lane_head = lax.broadcasted_iota(jnp.int32, (LANES, 512), 1) // HEAD_DIM
    krow = lax.broadcasted_iota(jnp.int32, (LANES, 512), 0)
    ti = lax.broadcasted_iota(jnp.int32, (MXU, MXU), 0)
    tj = lax.broadcasted_iota(jnp.int32, (MXU, MXU), 1)
    same_chunk = (ti // C_CHUNK) == (tj // C_CHUNK)
    for d in range(2):
        e_beta = jnp.where(krow == lane_head + d * C_HEADS, 1.0, 0.0).astype(BF16)
        e_g = jnp.where(krow == lane_head + (2 + d) * C_HEADS, 1.0, 0.0).astype(BF16)
        beta_scr[d] = _dot_unit_sel(beta_all, e_beta)
        tri = jnp.where(same_chunk & ((tj <= ti) if d == 0 else (tj >= ti)), 1.0, 0.0).astype(BF16)
        for r in range(0, C_TILE, MXU):
            gi_scr[d, r:r + MXU, :] = _dot_data_sel(_dot_sel_data(tri, g_all[r:r + MXU]), e_g)

    ii = lax.broadcasted_iota(jnp.int32, (C_CHUNK, 512), 0)
    jj = lax.broadcasted_iota(jnp.int32, (C_CHUNK, 512), 1) % C_CHUNK
    ig = lax.broadcasted_iota(jnp.int32, (C_CHUNK, GROUP), 0)
    jg = lax.broadcasted_iota(jnp.int32, (C_CHUNK, GROUP), 1) % C_CHUNK
    incl = (ii >= jj, ii <= jj)
    incl_g = (ig >= jg, ig <= jg)
    strict_g = (ig > jg, ig < jg)
    eye = jnp.where(ig == jg, 1.0, 0.0)
    diag16 = (ig // 16) == (jg // 16)
    headmask = ones
    zero_rows = jnp.zeros((LANES - C_CHUNK, LANES), F32)

    def bd(qs):
        return [_blockdiag(q, headmask) for q in qs]

    def mm(ps, bds):
        return [_dot(p.astype(BF16), b) for p, b in zip(ps, bds)]

    def mm2(ps, rs, bds):
        both = [_dot(jnp.concatenate([p, r], axis=0).astype(BF16), b) for p, r, b in zip(ps, rs, bds)]
        return [x[:C_CHUNK] for x in both], [x[C_CHUNK:] for x in both]

    def add(xs, ys):
        return [x + y for x, y in zip(xs, ys)]

    def unit_inverses(ls):
        dg = [jnp.where(diag16, l, 0.0) for l in ls]
        og = [l - d for l, d in zip(ls, dg)]
        di = [eye - d for d in dg]
        d2 = mm(dg, bd(dg))
        d4, t = mm2(d2, di, bd(d2))
        di = add(di, t)
        d8, t = mm2(d4, di, bd(d4))
        di = add(di, t)
        di = add(di, mm(di, bd(d8)))
        nn = mm(di, bd(og))
        m = [eye - x for x in nn]
        nbd = bd(nn)
        m = add(m, mm(mm(m, nbd), nbd))
        return mm(m, bd(di))

    def transposed_decay(gi_):
        parts = []
        for p in range(512 // LANES):
            t = jnp.concatenate([gi_[:, p * LANES:(p + 1) * LANES], zero_rows], axis=0).T
            parts.append(t[:C_CHUNK, :] + pltpu.roll(t[C_CHUNK:, :], HEAD_DIM, 1))
        return jnp.concatenate(parts, axis=1)

    def chunks(cs):
        lms, vbs, kbgs, dest = [], [], [], []
        for c in cs:
            rows = pl.ds(pl.multiple_of(c * C_CHUNK, C_CHUNK), C_CHUNK)
            qc, kc, vc = (qkv_ref[0, rows, i * C_WIDTH:(i + 1) * C_WIDTH] for i in range(3))
            kk, qk = [], []
            for gidx in range(2):
                sl = slice(gidx * GROUP, (gidx + 1) * GROUP)
                kbd = _blockdiag(kc[:, sl], headmask)
                kk.append(_dot_nt(kc[:, sl].astype(BF16), kbd))
                qk.append(_dot_nt(qc[:, sl].astype(BF16), kbd))
            for d in range(2):
                u_ref, w_ref, qk_ref, qg_ref, kd_ref, gl_ref = outs[d]
                beta = beta_scr[d, rows, :]
                gi_ = gi_scr[d, rows, :]
                decay = jnp.exp(jnp.where(incl[d], gi_ - transposed_decay(gi_), NEG_INF))
                eg = jnp.exp(gi_)
                last = C_CHUNK - 1 if d == 0 else 0
                glrow = gi_[last:last + 1, :]
                kd_ref[0, rows, :] = (kc * jnp.exp(glrow - gi_)).astype(BF16)
                qg_ref[0, rows, :] = (qc * eg).astype(BF16)
                gl_ref[0, pl.ds(c, 1), 0, :] = jnp.exp(glrow)
                kbg = kc * beta * eg
                vb = vc * beta
                for gidx in range(2):
                    sl = slice(gidx * GROUP, (gidx + 1) * GROUP)
                    lms.append(jnp.where(strict_g[d], kk[gidx] * beta[:, sl] * decay[:, sl], 0.0))
                    qk_ref[0, rows, sl] = jnp.where(incl_g[d], qk[gidx] * decay[:, sl], 0.0).astype(BF16)
                    vbs.append(vb[:, sl])
                    kbgs.append(kbg[:, sl])
                    dest.append((u_ref, w_ref, rows, sl))
        tinvs = unit_inverses(lms)
        uw = mm(tinvs, [jnp.concatenate([a, b], axis=1) for a, b in zip(bd(vbs), bd(kbgs))])
        for (u_ref, w_ref, rows, sl), x in zip(dest, uw):
            u_ref[0, rows, sl] = x[:, :GROUP]
            w_ref[0, rows, sl] = x[:, GROUP:].astype(BF16)

    def body(c2, carry):
        chunks((2 * c2, 2 * c2 + 1))
        return carry

    lax.fori_loop(0, N_CHUNK // 2, body, 0)


def _delta_chunks(qkv, ba, alog_row, dtb_row):
    batch, seq, _ = qkv.shape
    nt = seq // C_TILE
    tile = lambda b, n: (b, n, 0)
    wide = lambda dt: jax.ShapeDtypeStruct((batch, seq, C_WIDTH), dt)
    gl_sds = jax.ShapeDtypeStruct((batch, seq // C_CHUNK, 1, C_WIDTH), F32)
    out_shape = [wide(F32), wide(BF16), wide(BF16), wide(BF16), wide(BF16), gl_sds] * 2
    wide_spec = pl.BlockSpec((1, C_TILE, C_WIDTH), tile)
    gl_spec = pl.BlockSpec((1, N_CHUNK, 1, C_WIDTH), lambda b, n: (b, n, 0, 0))
    return pl.pallas_call(
        _delta_chunks_kernel,
        name="delta_chunks",
        grid=(batch, nt),
        in_specs=[
            pl.BlockSpec((1, C_TILE, K1_C), tile),
            pl.BlockSpec((1, C_TILE, K1_BA), tile),
            _const_spec((1, LANES)), _const_spec((1, LANES))],
        out_specs=[wide_spec] * 5 + [gl_spec] + [wide_spec] * 5 + [gl_spec],
        out_shape=out_shape,
        scratch_shapes=[
            pltpu.VMEM((2, C_TILE, C_WIDTH), F32),
            pltpu.VMEM((2, C_TILE, C_WIDTH), F32),
        ],
        compiler_params=_params("parallel", "parallel"),
    )(qkv, ba, alog_row, dtb_row)


def _delta_scan_kernel(uf_ref, wf_ref, qkf_ref, qgf_ref, kdf_ref, glf_ref,
                       ub_ref, wb_ref, qkb_ref, qgb_ref, kdb_ref, glb_ref, of_ref, ob_ref, state_scr):
    n = pl.program_id(1)
    bblk = uf_ref.shape[0]
    ins = ((uf_ref, wf_ref, qkf_ref, qgf_ref, kdf_ref, glf_ref, of_ref),
           (ub_ref, wb_ref, qkb_ref, qgb_ref, kdb_ref, glb_ref, ob_ref))

    @pl.when(n == 0)
    def _():
        state_scr[...] = jnp.zeros_like(state_scr)

    headmask = _head_mask()
    head01 = _head_ones(GROUP)

    def body(c, carry):
        chains = []
        for d in range(2):
            cc = c if d == 0 else SCAN_CHUNKS - 1 - c
            rows = pl.ds(pl.multiple_of(cc * C_CHUNK, C_CHUNK), C_CHUNK)
            for b in range(bblk):
                for gidx in range(2):
                    chains.append((d, b, gidx, cc, rows, slice(gidx * GROUP, (gidx + 1) * GROUP)))
        states = [state_scr[b, d, gidx] for d, b, gidx, _, _, _ in chains]
        both = [_dot(jnp.concatenate([ins[d][1][b, rows, sl], ins[d][3][b, rows, sl]], axis=0), st.astype(BF16))
                for (d, b, _, _, rows, sl), st in zip(chains, states)]
        v_new = [ins[d][0][b, rows, sl] - bo[:C_CHUNK] for (d, b, _, _, rows, sl), bo in zip(chains, both)]
        intra = [_dot(ins[d][2][b, rows, sl], _blockdiag(v, head01)) for (d, b, _, _, rows, sl), v in zip(chains, v_new)]
        upd = [_dot_tn(ins[d][4][b, rows, sl], v.astype(BF16)) for (d, b, _, _, rows, sl), v in zip(chains, v_new)]
        for (d, b, gidx, cc, rows, sl), st, bo, it, up in zip(chains, states, both, intra, upd):
            ins[d][6][b, rows, sl] = bo[C_CHUNK:] + it
            gl = ins[d][5][b, pl.ds(cc, 1), 0, :]
            state_scr[b, d, gidx] = st * gl[:, sl] + jnp.where(headmask, up, 0.0)
        return carry

    lax.fori_loop(0, SCAN_CHUNKS, body, 0)


def _delta_scan(factors):
    batch, seq, _ = factors[0].shape
    nt = seq // SCAN_TILE
    bblk = min(batch, 4)
    fwd = lambda b, n: (b, n, 0)
    bwd = lambda b, n: (b, nt - 1 - n, 0)

    def specs(tile):
        wide = pl.BlockSpec((bblk, SCAN_TILE, C_WIDTH), tile)
        gl = pl.BlockSpec((bblk, SCAN_CHUNKS, 1, C_WIDTH), lambda b, n: tile(b, n) + (0,))
        return [wide] * 5 + [gl]

    out_sds = jax.ShapeDtypeStruct((batch, seq, C_WIDTH), F32)
    return pl.pallas_call(
        _delta_scan_kernel,
        name="delta_scan",
        grid=(batch // bblk, nt),
        in_specs=specs(fwd) + specs(bwd),
        out_specs=[pl.BlockSpec((bblk, SCAN_TILE, C_WIDTH), fwd), pl.BlockSpec((bblk, SCAN_TILE, C_WIDTH), bwd)],
        out_shape=[out_sds, out_sds],
        scratch_shapes=[pltpu.VMEM((bblk, 2, 2, GROUP, GROUP), F32)],
        compiler_params=_params("parallel", "arbitrary"),
    )(*factors)


def _merge_kernel(x_ref, n1_ref, oa0, la0, oa1, la1, oa2, la2, ob_ref, ocf_ref, ocb_ref, z_ref, ogain_ref,
                  wg_ref, wba_ref, wbb_ref, wbc_ref, wo_ref, out_ref, inter_scr):
    tm = x_ref.shape[0]
    x = x_ref[...]
    h = _rms_rows(x, n1_ref[...]).astype(BF16)

    d1, d2 = A_DILATIONS[1], A_DILATIONS[2]
    ones = _head_ones(MXU)
    n_col = D_MODEL // MXU

    def interleaved(ref, d, slot, lanes):
        for r in range(d):
            inter_scr[slot, pl.ds(r, tm // d, stride=d), :] = ref[0, r, :, lanes]
        return inter_scr[slot]

    def mix_a(c):
        lanes = slice(c * LANES, (c + 1) * LANES)
        l0, o0 = la0[:, lanes], oa0[:, lanes]
        l1, o1 = interleaved(la1, d1, c, lanes), interleaved(oa1, d1, 4 + c, lanes)
        l2, o2 = interleaved(la2, d2, 8 + c, lanes), interleaved(oa2, d2, 12 + c, lanes)
        m = jnp.maximum(jnp.maximum(l0, l1), l2)
        e0, e1, e2 = jnp.exp2(l0 - m), jnp.exp2(l1 - m), jnp.exp2(l2 - m)
        return ((e0 * o0 + e1 * o1 + e2 * o2) / (e0 + e1 + e2)).astype(BF16)

    def norm_c(c):
        lanes = slice(c * MXU, (c + 1) * MXU)
        blk = ocf_ref[:, lanes] + ocb_ref[:, lanes]
        ss = _dot((blk * blk).astype(BF16), ones)
        zz = z_ref[:, lanes]
        return (blk * lax.rsqrt(ss * (1.0 / HEAD_DIM) + EPS) * ogain_ref[:, lanes] * (zz * _sigmoid(zz))).astype(BF16)

    def term(br, o_br, w_ref, c):
        gate = _sigmoid(_dot(h, wg_ref[:, br * D_MODEL + c * MXU:br * D_MODEL + (c + 1) * MXU]))
        return gate * _dot(o_br, w_ref[:, c * MXU:(c + 1) * MXU])

    o_b = ob_ref[...]
    term_b, oa_tiles = [], []
    for c in range(n_col):
        term_b.append(term(1, o_b, wbb_ref, c))
        oa_tiles.append(mix_a(c))
    o_a = jnp.concatenate(oa_tiles, axis=1)
    merged, oc_tiles = [], []
    for c in range(n_col):
        merged.append(term(0, o_a, wba_ref, c) + term_b[c])
        if c < C_WIDTH // MXU:
            oc_tiles.append(norm_c(c))
    o_c = jnp.concatenate(oc_tiles, axis=1)
    merged = [m + term(2, o_c, wbc_ref, c) for c, m in enumerate(merged)]
    out_ref[...] = x + _dot(jnp.concatenate(merged, axis=1).astype(BF16), wo_ref[...])


def _merge(x2, n1, a_res, o_b, o_cf, o_cb, z, ogain, wg, wba, wbb, wbc, wo, seq):
    tm = TOKEN_TILE
    t = x2.shape[0]
    ns = seq // tm
    row = lambda i: (i, 0)
    res = lambda i: (i // ns, 0, i % ns, 0)
    wide = pl.BlockSpec((tm, D_MODEL), row)
    half = pl.BlockSpec((tm, 512), row)
    d1, d2 = A_DILATIONS[1], A_DILATIONS[2]
    res1 = pl.BlockSpec((1, d1, tm // d1, 512), res)
    res2 = pl.BlockSpec((1, d2, tm // d2, 512), res)
    (o0, l0), (o1, l1), (o2, l2) = a_res
    return pl.pallas_call(
        _merge_kernel,
        name="merge",
        grid=(t // tm,),
        in_specs=[wide, _const_spec((1, D_MODEL)), half, half, res1, res1, res2, res2] + [half] * 4 + [
            _const_spec((1, C_WIDTH)), _const_spec((D_MODEL, N_BRANCH * D_MODEL)), _const_spec((512, D_MODEL)),
            _const_spec((512, D_MODEL)), _const_spec((512, D_MODEL)), _const_spec((D_MODEL, D_MODEL))],
        out_specs=wide,
        out_shape=jax.ShapeDtypeStruct((t, D_MODEL), F32),
        scratch_shapes=[pltpu.VMEM((16, tm, LANES), F32)],
        compiler_params=_params("parallel"),
    )(x2, n1, o0.reshape(t, 512), l0.reshape(t, 512), o1, l1, o2, l2, o_b, o_cf, o_cb, z, ogain, wg, wba, wbb, wbc, wo)


def _ffn_kernel(x_ref, n2_ref, wi_ref, wo_ref, out_ref):
    x = x_ref[...]
    h = _rms_rows(x, n2_ref[...]).astype(BF16)
    acc = x
    for c in range(0, FFN_HIDDEN, MXU):
        gate = _dot(h, wi_ref[:, c:c + MXU])
        up = _dot(h, wi_ref[:, FFN_HIDDEN + c:FFN_HIDDEN + c + MXU])
        act = (gate * _sigmoid(gate) * up).astype(BF16)
        acc = acc + _dot(act, wo_ref[c:c + MXU, :])
    out_ref[...] = acc


def _ffn(x2, n2, wi, wo):
    tm = TOKEN_TILE
    t = x2.shape[0]
    row = lambda i: (i, 0)
    return pl.pallas_call(
        _ffn_kernel,
        name="ffn",
        grid=(t // tm,),
        in_specs=[pl.BlockSpec((tm, D_MODEL), row), _const_spec((1, D_MODEL)),
                  _const_spec((D_MODEL, 2 * FFN_HIDDEN)), _const_spec((FFN_HIDDEN, D_MODEL))],
        out_specs=pl.BlockSpec((tm, D_MODEL), row),
        out_shape=jax.ShapeDtypeStruct((t, D_MODEL), F32),
        compiler_params=_params("parallel"),
    )(x2, n2, wi, wo)


def _rope_tables(seq):
    pos = jnp.arange(seq, dtype=F32)
    inv_freq = jnp.power(jnp.float32(ROPE_THETA), -jnp.arange(0, ROPE_DIM, 2, dtype=F32) / ROPE_DIM)
    ang = pos[:, None] * inv_freq[None, :]
    cos, sin = jnp.cos(ang), jnp.sin(ang)
    half = ROPE_DIM // 2
    one = jnp.ones((seq, HEAD_DIM - ROPE_DIM), F32)
    zero = jnp.zeros((seq, HEAD_DIM - half), F32)
    cos_h = jnp.concatenate([cos, cos, one], axis=1)
    sa_h = jnp.concatenate([-sin, zero], axis=1)
    sb_h = jnp.concatenate([jnp.zeros((seq, half), F32), sin, zero[:, half:]], axis=1)
    tile = lambda t: jnp.concatenate([t, t], axis=1)
    return tile(cos_h), tile(sa_h), tile(sb_h)


def _pack_layer(l, norm1, w_in, qk_gain, sink, conv_w, a_log, dt_bias, o_gain, w_gate, w_br_a, w_br_b, w_br_c,
                w_out, norm2, w_ffn_in, w_ffn_out):
    wi = w_in[l]
    kb0, kb1 = wi[:, 5120:5184], wi[:, 5184:5248]
    vb0, vb1 = wi[:, 5248:5312], wi[:, 5312:5376]
    w1 = jnp.concatenate([
        wi[:, :5120], kb0, kb0, kb1, kb1, vb0, vb0, vb1, vb1, wi[:, 5376:7424],
        wi[:, 7424:7456], jnp.zeros((D_MODEL, K1_BA - 32), F32)], axis=1).astype(BF16)
    scale = HEAD_DIM ** -0.5 * LOG2_E
    gains = jnp.stack([qk_gain[l, 0] * scale, qk_gain[l, 1], qk_gain[l, 2] * scale, qk_gain[l, 3]])
    gains = jnp.concatenate([gains, gains], axis=1)
    pad = lambda v: jnp.concatenate([jnp.zeros((16,), F32), v.reshape(-1), jnp.zeros((LANES - 32,), F32)])[None, :]
    return dict(
        n1=norm1[l][None, :], w1=w1, gains=gains, sink=sink[l] * LOG2_E, conv_w=conv_w[l],
        alog=pad(a_log[l]), dtb=pad(dt_bias[l]), ogain=jnp.tile(o_gain[l], C_HEADS)[None, :],
        wg=w_gate[l].astype(BF16), wba=w_br_a[l].astype(BF16), wbb=w_br_b[l].astype(BF16),
        wbc=w_br_c[l].astype(BF16), wo=w_out[l].astype(BF16), n2=norm2[l][None, :],
        wfi=w_ffn_in[l].astype(BF16), wfo=w_ffn_out[l].astype(BF16))


def _layer(x2, p, tabs, batch, seq):
    t = batch * seq
    a0, a1, a2, qkv_b, qkv_c, z, ba = _inproj(x2, p["n1"], p["w1"], *tabs, p["gains"], p["conv_w"], batch, seq)
    a_res = _mixer_a((a0.reshape(batch, 1, seq, A_QKV), a1, a2))
    o_b = _mixer_b(qkv_b, p["sink"], batch, seq)
    factors = _delta_chunks(qkv_c.reshape(batch, seq, K1_C), ba.reshape(batch, seq, K1_BA), p["alog"], p["dtb"])
    o_cf, o_cb = _delta_scan(factors)
    x2 = _merge(x2, p["n1"], a_res, o_b, o_cf.reshape(t, C_WIDTH), o_cb.reshape(t, C_WIDTH), z, p["ogain"],
                p["wg"], p["wba"], p["wbb"], p["wbc"], p["wo"], seq)
    return _ffn(x2, p["n2"], p["wfi"], p["wfo"])


def kernel(x_prompt, x_sample, norm1, w_in, qk_gain, sink, conv_w, a_log, dt_bias, o_gain, w_gate, w_br_a, w_br_b,
           w_br_c, w_out, norm2, w_ffn_in, w_ffn_out):
    depth = w_in.shape[0]
    layers = [_pack_layer(l, norm1, w_in, qk_gain, sink, conv_w, a_log, dt_bias, o_gain, w_gate, w_br_a, w_br_b,
                          w_br_c, w_out, norm2, w_ffn_in, w_ffn_out) for l in range(depth)]
    outs = []
    for x in (x_prompt, x_sample):
        batch, seq, _ = x.shape
        assert seq % (A_DILATIONS[-1] * 256) == 0
        tabs = _rope_tables(seq)
        x2 = x.reshape(batch * seq, D_MODEL)
        for p in layers:
            x2 = _layer(x2, p, tabs, batch, seq)
        outs.append(x2.reshape(batch, seq, D_MODEL))
    return tuple(outs)
```

```python
import functools

import jax
import jax.numpy as jnp
from jax import lax
from jax.experimental import pallas as pl
from jax.experimental.pallas import tpu as pltpu

F32 = jnp.float32
BF16 = jnp.bfloat16

D_MODEL = 1024
HEAD_DIM = 64
ROPE_DIM = HEAD_DIM // 4
ROPE_THETA = 500000.0
EPS = 1e-6
NEG_INF = -1e30
LOG2_E = 1.4426950408889634

A_PATTERNS = ((128, 1), (512, 4), (2048, 16))
A_DILATIONS = tuple(d for _, d in A_PATTERNS)
A_GROUPS = 3
A_WIDTH = 512
A_QKV = A_GROUPS * A_WIDTH
A_HALO = 64
B_HALF_WINDOW = 128
C_HEADS = 8
C_CHUNK = 64
C_CONV = 5
C_WIDTH = 512
FFN_HIDDEN = 2816
N_BRANCH = 3

LANES = 128
MXU = 256
VMEM_LIMIT = 56 * 1024 * 1024
TOKEN_TILE = 512
IN_HALO = 16

K1_A = 3 * A_QKV
K1_B = 512 + 256 + 256
K1_C = 3 * C_WIDTH
K1_Z = C_WIDTH
K1_BA = LANES
K1_OFF_B = K1_A
K1_OFF_C = K1_OFF_B + K1_B
K1_OFF_Z = K1_OFF_C + K1_C
K1_OFF_BA = K1_OFF_Z + K1_Z
K1_WIDTH = K1_OFF_BA + K1_BA


def _dot(a, b):
    return jnp.dot(a, b, preferred_element_type=F32)


def _dot_nt(a, b):
    return lax.dot_general(a, b, (((1,), (1,)), ((), ())), preferred_element_type=F32)


def _dot_tn(a, b):
    return lax.dot_general(a, b, (((0,), (0,)), ((), ())), preferred_element_type=F32)


def _split3(a):
    a1 = a.astype(BF16)
    r1 = a - a1.astype(F32)
    a2 = r1.astype(BF16)
    a3 = (r1 - a2.astype(F32)).astype(BF16)
    return a1, a2, a3


def _dot_data_sel(a, sel):
    a1, a2, a3 = _split3(a)
    return _dot(a1, sel) + _dot(a2, sel) + _dot(a3, sel)


def _dot_unit_sel(a, sel):
    a1, a2, _ = _split3(a)
    return _dot(a1, sel) + _dot(a2, sel)


def _dot_sel_data(sel, b):
    b1, b2, b3 = _split3(b)
    return _dot(sel, b1) + _dot(sel, b2) + _dot(sel, b3)


def _rms_rows(x, gain):
    ms = jnp.mean(x * x, axis=-1, keepdims=True)
    return x * lax.rsqrt(ms + EPS) * gain


def _sigmoid(x):
    return 0.5 * jnp.tanh(0.5 * x) + 0.5


def _head_ones(n):
    r = lax.broadcasted_iota(jnp.int32, (n, n), 0) // HEAD_DIM
    c = lax.broadcasted_iota(jnp.int32, (n, n), 1) // HEAD_DIM
    return jnp.where(r == c, 1.0, 0.0).astype(BF16)


def _const_spec(shape):
    nd = len(shape)
    return pl.BlockSpec(shape, lambda *_: (0,) * nd, pipeline_mode=pl.Buffered(1))


def _params(*sem):
    return pltpu.CompilerParams(dimension_semantics=sem, vmem_limit_bytes=VMEM_LIMIT)


def _inproj_kernel(x_ref, xp_ref, xn_ref, n1_ref, w_ref, cos_ref, sa_ref, sb_ref, gain_ref, convw_ref,
                   a0_ref, a1_ref, a2_ref, ob_ref, oc_ref, oz_ref, oba_ref, deint_scr, conv_scr, *, tiles_per_seq):
    tm = x_ref.shape[0]
    x_ext = jnp.concatenate([xp_ref[...], x_ref[...], xn_ref[...]], axis=0)
    h_ext = _rms_rows(x_ext, n1_ref[...]).astype(BF16)
    h = h_ext[IN_HALO:IN_HALO + tm]
    ones = _head_ones(MXU)
    tile_in_seq = pl.program_id(0) % tiles_per_seq
    ext_row = lax.broadcasted_iota(jnp.int32, (tm + 2 * IN_HALO, 1), 0)
    in_seq = (((ext_row >= IN_HALO) | (tile_in_seq > 0))
              & ((ext_row < IN_HALO + tm) | (tile_in_seq < tiles_per_seq - 1)))
    cos, sa, sb = cos_ref[...], sa_ref[...], sb_ref[...]
    a_refs = (a0_ref, a1_ref, a2_ref)
    slot = [0]

    def qk_norm_rope(acc, gain):
        ss = _dot((acc * acc).astype(BF16), ones)
        r = lax.rsqrt(ss * (1.0 / HEAD_DIM) + EPS)
        outs = []
        for s in range(0, MXU, LANES):
            y = acc[:, s:s + LANES] * r[:, s:s + LANES] * gain
            outs.append(y * cos + pltpu.roll(y, LANES - 8, 1) * sa + pltpu.roll(y, 8, 1) * sb)
        return outs

    def emit_a(g, col, val):
        d = A_DILATIONS[g]
        if d == 1:
            a0_ref[:, col:col + LANES] = val.astype(BF16)
            return
        s = slot[0]
        slot[0] = (s + 1) % deint_scr.shape[0]
        deint_scr[s] = val
        for r in range(d):
            a_refs[g][0, r, :, col:col + LANES] = deint_scr[s, pl.ds(r, tm // d, stride=d), :].astype(BF16)

    items = []

    def a_item(kind, g, c):
        def finish(acc):
            parts = qk_norm_rope(acc, gain_ref[kind:kind + 1, :]) if kind < 2 else [acc[:, :LANES], acc[:, LANES:]]
            for i, part in enumerate(parts):
                emit_a(g, kind * A_WIDTH + c + i * LANES, part)
        items.append((kind * A_QKV + g * A_WIDTH + c, MXU, finish))

    def qk_item(c0, gain_row, out_ref, o0):
        def finish(acc):
            for i, part in enumerate(qk_norm_rope(acc, gain_ref[gain_row:gain_row + 1, :])):
                out_ref[:, o0 + i * LANES:o0 + (i + 1) * LANES] = part.astype(out_ref.dtype)
        items.append((c0, MXU, finish))

    def plain_item(c0, width, out_ref, o0):
        def finish(acc):
            out_ref[:, o0:o0 + width] = acc.astype(out_ref.dtype)
        items.append((c0, width, finish))

    for kind in range(3):
        for g in range(A_GROUPS):
            for c in range(0, A_WIDTH, MXU):
                a_item(kind, g, c)
    for c in range(0, 512, MXU):
        qk_item(K1_OFF_B + c, 2, ob_ref, c)
    qk_item(K1_OFF_B + 512, 3, ob_ref, 512)
    plain_item(K1_OFF_B + 768, 256, ob_ref, 768)
    def conv_item(j):
        c = j * MXU

        def finish(acc):
            buf = j % conv_scr.shape[0]
            conv_scr[buf] = jnp.where(in_seq, acc, 0.0)
            y = None
            for tap in range(C_CONV):
                off = IN_HALO - C_CONV // 2 + tap
                term = conv_scr[buf, off:off + tm, :] * convw_ref[tap:tap + 1, c:c + MXU]
                y = term if y is None else y + term
            y = y * _sigmoid(y)
            if c < 2 * C_WIDTH:
                ss = _dot((y * y).astype(BF16), ones)
                y = y * lax.rsqrt(ss + EPS) * (HEAD_DIM ** -0.5 if c < C_WIDTH else 1.0)
            oc_ref[:, c:c + MXU] = y
        items.append((K1_OFF_C + c, MXU, finish, h_ext))

    for c in range(0, K1_Z, MXU):
        plain_item(K1_OFF_Z + c, MXU, oz_ref, c)
    plain_item(K1_OFF_BA, K1_BA, oba_ref, 0)
    n_conv = K1_C // MXU
    gap = len(items) // n_conv
    for j in range(n_conv):
        conv_item(j)
        items.insert(j * (gap + 1) + gap // 2, items.pop())

    product = lambda it: _dot(it[3] if len(it) > 3 else h, w_ref[:, it[0]:it[0] + it[1]])
    pending, issued = {}, 0
    for i, it in enumerate(items):
        while issued <= min(i + (2 if len(it) > 3 else 1), len(items) - 1):
            pending[issued] = product(items[issued])
            issued += 1
        it[2](pending.pop(i))


def _inproj(x2, n1, w, cos_t, sa_t, sb_t, gains, conv_w, batch, seq):
    tm = TOKEN_TILE
    t = x2.shape[0]
    ns = seq // tm
    row = lambda i: (i, 0)
    pos = lambda i: (i % ns, 0)
    d1, d2 = A_DILATIONS[1], A_DILATIONS[2]
    res = lambda i: (i // ns, 0, i % ns, 0)
    per = tm // IN_HALO
    last = t // IN_HALO - 1
    return pl.pallas_call(
        functools.partial(_inproj_kernel, tiles_per_seq=ns),
        name="inproj",
        grid=(t // tm,),
        in_specs=[
            pl.BlockSpec((tm, D_MODEL), row),
            pl.BlockSpec((IN_HALO, D_MODEL), lambda i: (jnp.maximum(i * per - 1, 0), 0)),
            pl.BlockSpec((IN_HALO, D_MODEL), lambda i: (jnp.minimum((i + 1) * per, last), 0)),
            _const_spec((1, D_MODEL)),
            _const_spec((D_MODEL, K1_WIDTH)),
            pl.BlockSpec((tm, LANES), pos),
            pl.BlockSpec((tm, LANES), pos),
            pl.BlockSpec((tm, LANES), pos),
            _const_spec((4, LANES)),
            _const_spec((C_CONV, K1_C)),
        ],
        out_specs=[
            pl.BlockSpec((tm, A_QKV), row),
            pl.BlockSpec((1, d1, tm // d1, A_QKV), res),
            pl.BlockSpec((1, d2, tm // d2, A_QKV), res),
            pl.BlockSpec((tm, K1_B), row),
            pl.BlockSpec((tm, K1_C), row),
            pl.BlockSpec((tm, K1_Z), row),
            pl.BlockSpec((tm, K1_BA), row),
        ],
        out_shape=[
            jax.ShapeDtypeStruct((t, A_QKV), BF16),
            jax.ShapeDtypeStruct((batch, d1, seq // d1, A_QKV), BF16),
            jax.ShapeDtypeStruct((batch, d2, seq // d2, A_QKV), BF16),
            jax.ShapeDtypeStruct((t, K1_B), BF16),
            jax.ShapeDtypeStruct((t, K1_C), F32),
            jax.ShapeDtypeStruct((t, K1_Z), F32),
            jax.ShapeDtypeStruct((t, K1_BA), F32),
        ],
        scratch_shapes=[pltpu.VMEM((4, tm, LANES), F32), pltpu.VMEM((2, tm + 2 * IN_HALO, MXU), F32)],
        compiler_params=_params("parallel"),
    )(x2, x2, x2, n1, w, cos_t, sa_t, sb_t, gains, conv_w)


def _attn_kernel(*refs, tq, halo, sub, length, kv_shared, has_sink, has_lse):
    refs = list(refs)
    sink_ref = refs.pop(0) if has_sink else None
    q_ref, kp_ref, kc_ref, kn_ref, vp_ref, vc_ref, vn_ref, o_ref = refs[:8]
    rest = refs[8:]
    lse_ref = rest.pop(0) if has_lse else None
    kbuf, vbuf = rest
    n = pl.program_id(2)
    nk = sub + 2 * halo

    kbuf[0:halo, :] = kp_ref[0, 0]
    kbuf[halo:halo + tq, :] = kc_ref[0, 0]
    kbuf[halo + tq:, :] = kn_ref[0, 0]
    vbuf[0:halo, :] = vp_ref[0, 0]
    vbuf[halo:halo + tq, :] = vc_ref[0, 0]
    vbuf[halo + tq:, :] = vn_ref[0, 0]

    rows = lax.broadcasted_iota(jnp.int32, (2 * sub, nk), 0) % sub
    cols = lax.broadcasted_iota(jnp.int32, (2 * sub, nk), 1)
    rel = cols - halo - rows
    band = (rel <= halo) & (rel >= -halo)
    lane_low = lax.broadcasted_iota(jnp.int32, (2 * sub, LANES), 1) < HEAD_DIM
    row_first = lax.broadcasted_iota(jnp.int32, (2 * sub, LANES), 0) < sub
    own_lanes = lane_low == row_first
    low_lane = lax.broadcasted_iota(jnp.int32, (sub, LANES), 1) < HEAD_DIM
    first_row = lax.broadcasted_iota(jnp.int32, (2 * sub, 1), 0) < sub

    for s in range(tq // sub):
        kpos = n * tq + (s * sub - halo) + cols
        mask = band & (kpos >= 0) & (kpos < length)
        qrows = slice(s * sub, (s + 1) * sub)
        krows = slice(s * sub, s * sub + nk)
        kcol = [(hp // 2 if kv_shared else hp) * LANES for hp in range(4)]
        sc = []
        for hp in range(4):
            qp = q_ref[0, 0, qrows, hp * LANES:(hp + 1) * LANES]
            qm = jnp.where(own_lanes, jnp.concatenate([qp, qp], axis=0), jnp.zeros((), qp.dtype))
            sc.append(jnp.where(mask, _dot_nt(qm, kbuf[krows, kcol[hp]:kcol[hp] + LANES]), NEG_INF))
        m = [jnp.max(x, axis=-1, keepdims=True) for x in sc]
        if has_sink:
            snk = [jnp.where(first_row, sink_ref[2 * hp], sink_ref[2 * hp + 1]) for hp in range(4)]
            m = [jnp.maximum(x, y) for x, y in zip(m, snk)]
        p = [jnp.exp2(x - y) for x, y in zip(sc, m)]
        den = [jnp.sum(x, axis=-1, keepdims=True) for x in p]
        if has_sink:
            den = [x + jnp.exp2(y - z) for x, y, z in zip(den, snk, m)]
        o = [_dot(x.astype(BF16), vbuf[krows, kc0:kc0 + LANES]) for x, kc0 in zip(p, kcol)]
        for hp in range(4):
            lanes = slice(hp * LANES, (hp + 1) * LANES)
            den_p = jnp.where(low_lane, den[hp][:sub], den[hp][sub:])
            o_p = jnp.where(low_lane, o[hp][:sub], o[hp][sub:]) / den_p
            o_ref[0, 0, qrows, lanes] = o_p.astype(o_ref.dtype)
            if has_lse:
                m_p = jnp.where(low_lane, m[hp][:sub], m[hp][sub:])
                lse_ref[0, 0, qrows, lanes] = m_p + jnp.log2(den_p)


def _banded_attention(qkv, *, name, q_cb, k_cb, v_cb, kv_width, halo, tq, sub, kv_shared, sink=None,
                      has_lse=False, out_dtype=F32):
    batch, nres, length, _ = qkv.shape
    nb = length // tq
    per = tq // halo
    last = length // halo - 1
    q_spec = pl.BlockSpec((1, 1, tq, 512), lambda b, r, n: (b, r, n, q_cb))

    def kv_specs(cb):
        return [
            pl.BlockSpec((1, 1, halo, kv_width), lambda b, r, n: (b, r, jnp.maximum(n * per - 1, 0), cb)),
            pl.BlockSpec((1, 1, tq, kv_width), lambda b, r, n: (b, r, n, cb)),
            pl.BlockSpec((1, 1, halo, kv_width), lambda b, r, n: (b, r, jnp.minimum((n + 1) * per, last), cb)),
        ]

    in_specs = [q_spec] + kv_specs(k_cb) + kv_specs(v_cb)
    args = [qkv] * 7
    if sink is not None:
        in_specs = [pl.BlockSpec(memory_space=pltpu.SMEM)] + in_specs
        args = [sink] + args
    o_spec = pl.BlockSpec((1, 1, tq, 512), lambda b, r, n: (b, r, n, 0))
    out_specs = [o_spec]
    out_shape = [jax.ShapeDtypeStruct((batch, nres, length, 512), out_dtype)]
    if has_lse:
        out_specs.append(o_spec)
        out_shape.append(jax.ShapeDtypeStruct((batch, nres, length, 512), F32))
    kern = functools.partial(_attn_kernel, tq=tq, halo=halo, sub=sub, length=length, kv_shared=kv_shared,
                             has_sink=sink is not None, has_lse=has_lse)
    return pl.pallas_call(
        kern,
        name=name,
        grid=(batch, nres, nb),
        in_specs=in_specs,
        out_specs=out_specs,
        out_shape=out_shape,
        scratch_shapes=[pltpu.VMEM((tq + 2 * halo, kv_width), BF16), pltpu.VMEM((tq + 2 * halo, kv_width), BF16)],
        compiler_params=_params("parallel", "parallel", "parallel"),
    )(*args)


def _mixer_a(groups):
    res = []
    for g, qkv in enumerate(groups):
        res.append(_banded_attention(qkv, name=f"attn_a{g}", q_cb=0, k_cb=1, v_cb=2, kv_width=512, halo=A_HALO,
                                     tq=min(1024, qkv.shape[2]), sub=128, kv_shared=False, has_lse=True))
    return res


def _mixer_b(qkv_b, sink, batch, seq):
    (o,) = _banded_attention(
        qkv_b.reshape(batch, 1, seq, K1_B), name="attn_b", q_cb=0, k_cb=2, v_cb=3, kv_width=256,
        halo=B_HALF_WINDOW, tq=1024, sub=128, kv_shared=True, sink=sink, out_dtype=BF16)
    return o.reshape(batch * seq, 512)


C_TILE = 1024
SCAN_TILE = 256
GROUP = 256
N_CHUNK = C_TILE // C_CHUNK
SCAN_CHUNKS = SCAN_TILE // C_CHUNK


def _head_mask():
    r = lax.broadcasted_iota(jnp.int32, (GROUP, GROUP), 0) // HEAD_DIM
    c = lax.broadcasted_iota(jnp.int32, (GROUP, GROUP), 1) // HEAD_DIM
    return r == c


def _blockdiag(x, headmask):
    xb = x.astype(BF16)
    return jnp.concatenate([xb, xb, xb, xb], axis=0) * headmask


def _delta_chunks_kernel(qkv_ref, ba_ref, alog_ref, dtb_ref,
                         uf_ref, wf_ref, qkf_ref, qgf_ref, kdf_ref, glf_ref,
                         ub_ref, wb_ref, qkb_ref, qgb_ref, kdb_ref, glb_ref,
                         beta_scr, gi_scr):
    outs = ((uf_ref, wf_ref, qkf_ref, qgf_ref, kdf_ref, glf_ref), (ub_ref, wb_ref, qkb_ref, qgb_ref, kdb_ref, glb_ref))
    ones = _head_ones(GROUP)

    ba = ba_ref[0]
    beta_all = _sigmoid(ba)
    zz = ba + dtb_ref[...]
    g_all = -jnp.exp(alog_ref[...]) * (jnp.maximum(zz, 0.0) + jnp.log(1.0 + jnp.exp(-jnp.abs(zz))))
    lane_head = lax.broadcasted_iota(jnp.int32, (LANES, 512), 1) // HEAD_DIM
    krow = lax.broadcasted_iota(jnp.int32, (LANES, 512), 0)
    ti = lax.broadcasted_iota(jnp.int32, (MXU, MXU), 0)
    tj = lax.broadcasted_iota(jnp.int32, (MXU, MXU), 1)
    same_chunk = (ti // C_CHUNK) == (tj // C_CHUNK)
    for d in range(2):
        e_beta = jnp.where(krow == lane_head + d * C_HEADS, 1.0, 0.0).astype(BF16)
        e_g = jnp.where(krow == lane_head + (2 + d) * C_HEADS, 1.0, 0.0).astype(BF16)
        beta_scr[d] = _dot_unit_sel(beta_all, e_beta)
        tri = jnp.where(same_chunk & ((tj <= ti) if d == 0 else (tj >= ti)), 1.0, 0.0).astype(BF16)
        for r in range(0, C_TILE, MXU):
            gi_scr[d, r:r + MXU, :] = _dot_data_sel(_dot_sel_data(tri, g_all[r:r + MXU]), e_g)

    ii = lax.broadcasted_iota(jnp.int32, (C_CHUNK, 512), 0)
    jj = lax.broadcasted_iota(jnp.int32, (C_CHUNK, 512), 1) % C_CHUNK
    ig = lax.broadcasted_iota(jnp.int32, (C_CHUNK, GROUP), 0)
    jg = lax.broadcasted_iota(jnp.int32, (C_CHUNK, GROUP), 1) % C_CHUNK
    incl = (ii >= jj, ii <= jj)
    incl_g = (ig >= jg, ig <= jg)
    strict_g = (ig > jg, ig < jg)
    eye = jnp.where(ig == jg, 1.0, 0.0)
    diag16 = (ig // 16) == (jg // 16)
    headmask = ones
    zero_rows = jnp.zeros((LANES - C_CHUNK, LANES), F32)

    def bd(qs):
        return [_blockdiag(q, headmask) for q in qs]

    def mm(ps, bds):
        return [_dot(p.astype(BF16), b) for p, b in zip(ps, bds)]

    def mm2(ps, rs, bds):
        both = [_dot(jnp.concatenate([p, r], axis=0).astype(BF16), b) for p, r, b in zip(ps, rs, bds)]
        return [x[:C_CHUNK] for x in both], [x[C_CHUNK:] for x in both]

    def add(xs, ys):
        return [x + y for x, y in zip(xs, ys)]

    def unit_inverses(ls):
        dg = [jnp.where(diag16, l, 0.0) for l in ls]
        og = [l - d for l, d in zip(ls, dg)]
        di = [eye - d for d in dg]
        d2 = mm(dg, bd(dg))
        d4, t = mm2(d2, di, bd(d2))
        di = add(di, t)
        d8, t = mm2(d4, di, bd(d4))
        di = add(di, t)
        di = add(di, mm(di, bd(d8)))
        nn = mm(di, bd(og))
        m = [eye - x for x in nn]
        nbd = bd(nn)
        m = add(m, mm(mm(m, nbd), nbd))
        return mm(m, bd(di))

    def transposed_decay(gi_):
        parts = []
        for p in range(512 // LANES):
            t = jnp.concatenate([gi_[:, p * LANES:(p + 1) * LANES], zero_rows], axis=0).T
            parts.append(t[:C_CHUNK, :] + pltpu.roll(t[C_CHUNK:, :], HEAD_DIM, 1))
        return jnp.concatenate(parts, axis=1)

    def chunks(cs):
        lms, vbs, kbgs, dest = [], [], [], []
        for c in cs:
            rows = pl.ds(pl.multiple_of(c * C_CHUNK, C_CHUNK), C_CHUNK)
            qc, kc, vc = (qkv_ref[0, rows, i * C_WIDTH:(i + 1) * C_WIDTH] for i in range(3))
            kk, qk = [], []
            for gidx in range(2):
                sl = slice(gidx * GROUP, (gidx + 1) * GROUP)
                kbd = _blockdiag(kc[:, sl], headmask)
                kk.append(_dot_nt(kc[:, sl].astype(BF16), kbd))
                qk.append(_dot_nt(qc[:, sl].astype(BF16), kbd))
            for d in range(2):
                u_ref, w_ref, qk_ref, qg_ref, kd_ref, gl_ref = outs[d]
                beta = beta_scr[d, rows, :]
                gi_ = gi_scr[d, rows, :]
                decay = jnp.exp(jnp.where(incl[d], gi_ - transposed_decay(gi_), NEG_INF))
                eg = jnp.exp(gi_)
                last = C_CHUNK - 1 if d == 0 else 0
                glrow = gi_[last:last + 1, :]
                kd_ref[0, rows, :] = (kc * jnp.exp(glrow - gi_)).astype(BF16)
                qg_ref[0, rows, :] = (qc * eg).astype(BF16)
                gl_ref[0, pl.ds(c, 1), 0, :] = jnp.exp(glrow)
                kbg = kc * beta * eg
                vb = vc * beta
                for gidx in range(2):
                    sl = slice(gidx * GROUP, (gidx + 1) * GROUP)
                    lms.append(jnp.where(strict_g[d], kk[gidx] * beta[:, sl] * decay[:, sl], 0.0))
                    qk_ref[0, rows, sl] = jnp.where(incl_g[d], qk[gidx] * decay[:, sl], 0.0).astype(BF16)
                    vbs.append(vb[:, sl])
                    kbgs.append(kbg[:, sl])
                    dest.append((u_ref, w_ref, rows, sl))
        tinvs = unit_inverses(lms)
        uw = mm(tinvs, [jnp.concatenate([a, b], axis=1) for a, b in zip(bd(vbs), bd(kbgs))])
        for (u_ref, w_ref, rows, sl), x in zip(dest, uw):
            u_ref[0, rows, sl] = x[:, :GROUP]
            w_ref[0, rows, sl] = x[:, GROUP:].astype(BF16)

    def body(c2, carry):
        chunks((2 * c2, 2 * c2 + 1))
        return carry

    lax.fori_loop(0, N_CHUNK // 2, body, 0)


def _delta_chunks(qkv, ba, alog_row, dtb_row):
    batch, seq, _ = qkv.shape
    nt = seq // C_TILE
    tile = lambda b, n: (b, n, 0)
    wide = lambda dt: jax.ShapeDtypeStruct((batch, seq, C_WIDTH), dt)
    gl_sds = jax.ShapeDtypeStruct((batch, seq // C_CHUNK, 1, C_WIDTH), F32)
    out_shape = [wide(F32), wide(BF16), wide(BF16), wide(BF16), wide(BF16), gl_sds] * 2
    wide_spec = pl.BlockSpec((1, C_TILE, C_WIDTH), tile)
    gl_spec = pl.BlockSpec((1, N_CHUNK, 1, C_WIDTH), lambda b, n: (b, n, 0, 0))
    return pl.pallas_call(
        _delta_chunks_kernel,
        name="delta_chunks",
        grid=(batch, nt),
        in_specs=[
            pl.BlockSpec((1, C_TILE, K1_C), tile),
            pl.BlockSpec((1, C_TILE, K1_BA), tile),
            _const_spec((1, LANES)), _const_spec((1, LANES))],
        out_specs=[wide_spec] * 5 + [gl_spec] + [wide_spec] * 5 + [gl_spec],
        out_shape=out_shape,
        scratch_shapes=[
            pltpu.VMEM((2, C_TILE, C_WIDTH), F32),
            pltpu.VMEM((2, C_TILE, C_WIDTH), F32),
        ],
        compiler_params=_params("parallel", "parallel"),
    )(qkv, ba, alog_row, dtb_row)


def _delta_scan_kernel(uf_ref, wf_ref, qkf_ref, qgf_ref, kdf_ref, glf_ref,
                       ub_ref, wb_ref, qkb_ref, qgb_ref, kdb_ref, glb_ref, of_ref, ob_ref, state_scr):
    n = pl.program_id(1)
    bblk = uf_ref.shape[0]
    ins = ((uf_ref, wf_ref, qkf_ref, qgf_ref, kdf_ref, glf_ref, of_ref),
           (ub_ref, wb_ref, qkb_ref, qgb_ref, kdb_ref, glb_ref, ob_ref))

    @pl.when(n == 0)
    def _():
        state_scr[...] = jnp.zeros_like(state_scr)

    headmask = _head_mask()
    head01 = _head_ones(GROUP)

    def body(c, carry):
        chains = []
        for d in range(2):
            cc = c if d == 0 else SCAN_CHUNKS - 1 - c
            rows = pl.ds(pl.multiple_of(cc * C_CHUNK, C_CHUNK), C_CHUNK)
            for b in range(bblk):
                for gidx in range(2):
                    chains.append((d, b, gidx, cc, rows, slice(gidx * GROUP, (gidx + 1) * GROUP)))
        states = [state_scr[b, d, gidx] for d, b, gidx, _, _, _ in chains]
        both = [_dot(jnp.concatenate([ins[d][1][b, rows, sl], ins[d][3][b, rows, sl]], axis=0), st.astype(BF16))
                for (d, b, _, _, rows, sl), st in zip(chains, states)]
        v_new = [ins[d][0][b, rows, sl] - bo[:C_CHUNK] for (d, b, _, _, rows, sl), bo in zip(chains, both)]
        intra = [_dot(ins[d][2][b, rows, sl], _blockdiag(v, head01)) for (d, b, _, _, rows, sl), v in zip(chains, v_new)]
        upd = [_dot_tn(ins[d][4][b, rows, sl], v.astype(BF16)) for (d, b, _, _, rows, sl), v in zip(chains, v_new)]
        for (d, b, gidx, cc, rows, sl), st, bo, it, up in zip(chains, states, both, intra, upd):
            ins[d][6][b, rows, sl] = bo[C_CHUNK:] + it
            gl = ins[d][5][b, pl.ds(cc, 1), 0, :]
            state_scr[b, d, gidx] = st * gl[:, sl] + jnp.where(headmask, up, 0.0)
        return carry

    lax.fori_loop(0, SCAN_CHUNKS, body, 0)


def _delta_scan(factors):
    batch, seq, _ = factors[0].shape
    nt = seq // SCAN_TILE
    bblk = min(batch, 4)
    fwd = lambda b, n: (b, n, 0)
    bwd = lambda b, n: (b, nt - 1 - n, 0)

    def specs(tile):
        wide = pl.BlockSpec((bblk, SCAN_TILE, C_WIDTH), tile)
        gl = pl.BlockSpec((bblk, SCAN_CHUNKS, 1, C_WIDTH), lambda b, n: tile(b, n) + (0,))
        return [wide] * 5 + [gl]

    out_sds = jax.ShapeDtypeStruct((batch, seq, C_WIDTH), F32)
    return pl.pallas_call(
        _delta_scan_kernel,
        name="delta_scan",
        grid=(batch // bblk, nt),
        in_specs=specs(fwd) + specs(bwd),
        out_specs=[pl.BlockSpec((bblk, SCAN_TILE, C_WIDTH), fwd), pl.BlockSpec((bblk, SCAN_TILE, C_WIDTH), bwd)],
        out_shape=[out_sds, out_sds],
        scratch_shapes=[pltpu.VMEM((bblk, 2, 2, GROUP, GROUP), F32)],
        compiler_params=_params("parallel", "arbitrary"),
    )(*factors)


def _merge_kernel(x_ref, n1_ref, oa0, la0, oa1, la1, oa2, la2, ob_ref, ocf_ref, ocb_ref, z_ref, ogain_ref,
                  wg_ref, wba_ref, wbb_ref, wbc_ref, wo_ref, out_ref, inter_scr):
    tm = x_ref.shape[0]
    x = x_ref[...]
    h = _rms_rows(x, n1_ref[...]).astype(BF16)

    d1, d2 = A_DILATIONS[1], A_DILATIONS[2]
    ones = _head_ones(MXU)
    n_col = D_MODEL // MXU

    def interleaved(ref, d, slot, lanes):
        for r in range(d):
            inter_scr[slot, pl.ds(r, tm // d, stride=d), :] = ref[0, r, :, lanes]
        return inter_scr[slot]

    def mix_a(c):
        lanes = slice(c * LANES, (c + 1) * LANES)
        l0, o0 = la0[:, lanes], oa0[:, lanes]
        l1, o1 = interleaved(la1, d1, c, lanes), interleaved(oa1, d1, 4 + c, lanes)
        l2, o2 = interleaved(la2, d2, 8 + c, lanes), interleaved(oa2, d2, 12 + c, lanes)
        m = jnp.maximum(jnp.maximum(l0, l1), l2)
        e0, e1, e2 = jnp.exp2(l0 - m), jnp.exp2(l1 - m), jnp.exp2(l2 - m)
        return ((e0 * o0 + e1 * o1 + e2 * o2) / (e0 + e1 + e2)).astype(BF16)

    def norm_c(c):
        lanes = slice(c * MXU, (c + 1) * MXU)
        blk = ocf_ref[:, lanes] + ocb_ref[:, lanes]
        ss = _dot((blk * blk).astype(BF16), ones)
        zz = z_ref[:, lanes]
        return (blk * lax.rsqrt(ss * (1.0 / HEAD_DIM) + EPS) * ogain_ref[:, lanes] * (zz * _sigmoid(zz))).astype(BF16)

    def term(br, o_br, w_ref, c):
        gate = _sigmoid(_dot(h, wg_ref[:, br * D_MODEL + c * MXU:br * D_MODEL + (c + 1) * MXU]))
        return gate * _dot(o_br, w_ref[:, c * MXU:(c + 1) * MXU])

    o_b = ob_ref[...]
    term_b, oa_tiles = [], []
    for c in range(n_col):
        term_b.append(term(1, o_b, wbb_ref, c))
        oa_tiles.append(mix_a(c))
    o_a = jnp.concatenate(oa_tiles, axis=1)
    merged, oc_tiles = [], []
    for c in range(n_col):
        merged.append(term(0, o_a, wba_ref, c) + term_b[c])
        if c < C_WIDTH // MXU:
            oc_tiles.append(norm_c(c))
    o_c = jnp.concatenate(oc_tiles, axis=1)
    merged = [m + term(2, o_c, wbc_ref, c) for c, m in enumerate(merged)]
    out_ref[...] = x + _dot(jnp.concatenate(merged, axis=1).astype(BF16), wo_ref[...])


def _merge(x2, n1, a_res, o_b, o_cf, o_cb, z, ogain, wg, wba, wbb, wbc, wo, seq):
    tm = TOKEN_TILE
    t = x2.shape[0]
    ns = seq // tm
    row = lambda i: (i, 0)
    res = lambda i: (i // ns, 0, i % ns, 0)
    wide = pl.BlockSpec((tm, D_MODEL), row)
    half = pl.BlockSpec((tm, 512), row)
    d1, d2 = A_DILATIONS[1], A_DILATIONS[2]
    res1 = pl.BlockSpec((1, d1, tm // d1, 512), res)
    res2 = pl.BlockSpec((1, d2, tm // d2, 512), res)
    (o0, l0), (o1, l1), (o2, l2) = a_res
    return pl.pallas_call(
        _merge_kernel,
        name="merge",
        grid=(t // tm,),
        in_specs=[wide, _const_spec((1, D_MODEL)), half, half, res1, res1, res2, res2] + [half] * 4 + [
            _const_spec((1, C_WIDTH)), _const_spec((D_MODEL, N_BRANCH * D_MODEL)), _const_spec((512, D_MODEL)),
            _const_spec((512, D_MODEL)), _const_spec((512, D_MODEL)), _const_spec((D_MODEL, D_MODEL))],
        out_specs=wide,
        out_shape=jax.ShapeDtypeStruct((t, D_MODEL), F32),
        scratch_shapes=[pltpu.VMEM((16, tm, LANES), F32)],
        compiler_params=_params("parallel"),
    )(x2, n1, o0.reshape(t, 512), l0.reshape(t, 512), o1, l1, o2, l2, o_b, o_cf, o_cb, z, ogain, wg, wba, wbb, wbc, wo)


def _ffn_kernel(x_ref, n2_ref, wi_ref, wo_ref, out_ref):
    x = x_ref[...]
    h = _rms_rows(x, n2_ref[...]).astype(BF16)
    acc = x
    for c in range(0, FFN_HIDDEN, MXU):
        gate = _dot(h, wi_ref[:, c:c + MXU])
        up = _dot(h, wi_ref[:, FFN_HIDDEN + c:FFN_HIDDEN + c + MXU])
        act = (gate * _sigmoid(gate) * up).astype(BF16)
        acc = acc + _dot(act, wo_ref[c:c + MXU, :])
    out_ref[...] = acc


def _ffn(x2, n2, wi, wo):
    tm = TOKEN_TILE
    t = x2.shape[0]
    row = lambda i: (i, 0)
    return pl.pallas_call(
        _ffn_kernel,
        name="ffn",
        grid=(t // tm,),
        in_specs=[pl.BlockSpec((tm, D_MODEL), row), _const_spec((1, D_MODEL)),
                  _const_spec((D_MODEL, 2 * FFN_HIDDEN)), _const_spec((FFN_HIDDEN, D_MODEL))],
        out_specs=pl.BlockSpec((tm, D_MODEL), row),
        out_shape=jax.ShapeDtypeStruct((t, D_MODEL), F32),
        compiler_params=_params("parallel"),
    )(x2, n2, wi, wo)


def _rope_tables(seq):
    pos = jnp.arange(seq, dtype=F32)
    inv_freq = jnp.power(jnp.float32(ROPE_THETA), -jnp.arange(0, ROPE_DIM, 2, dtype=F32) / ROPE_DIM)
    ang = pos[:, None] * inv_freq[None, :]
    cos, sin = jnp.cos(ang), jnp.sin(ang)
    half = ROPE_DIM // 2
    one = jnp.ones((seq, HEAD_DIM - ROPE_DIM), F32)
    zero = jnp.zeros((seq, HEAD_DIM - half), F32)
    cos_h = jnp.concatenate([cos, cos, one], axis=1)
    sa_h = jnp.concatenate([-sin, zero], axis=1)
    sb_h = jnp.concatenate([jnp.zeros((seq, half), F32), sin, zero[:, half:]], axis=1)
    tile = lambda t: jnp.concatenate([t, t], axis=1)
    return tile(cos_h), tile(sa_h), tile(sb_h)


def _pack_layer(l, norm1, w_in, qk_gain, sink, conv_w, a_log, dt_bias, o_gain, w_gate, w_br_a, w_br_b, w_br_c,
                w_out, norm2, w_ffn_in, w_ffn_out):
    wi = w_in[l]
    kb0, kb1 = wi[:, 5120:5184], wi[:, 5184:5248]
    vb0, vb1 = wi[:, 5248:5312], wi[:, 5312:5376]
    w1 = jnp.concatenate([
        wi[:, :5120], kb0, kb0, kb1, kb1, vb0, vb0, vb1, vb1, wi[:, 5376:7424],
        wi[:, 7424:7456], jnp.zeros((D_MODEL, K1_BA - 32), F32)], axis=1).astype(BF16)
    scale = HEAD_DIM ** -0.5 * LOG2_E
    gains = jnp.stack([qk_gain[l, 0] * scale, qk_gain[l, 1], qk_gain[l, 2] * scale, qk_gain[l, 3]])
    gains = jnp.concatenate([gains, gains], axis=1)
    pad = lambda v: jnp.concatenate([jnp.zeros((16,), F32), v.reshape(-1), jnp.zeros((LANES - 32,), F32)])[None, :]
    return dict(
        n1=norm1[l][None, :], w1=w1, gains=gains, sink=sink[l] * LOG2_E, conv_w=conv_w[l],
        alog=pad(a_log[l]), dtb=pad(dt_bias[l]), ogain=jnp.tile(o_gain[l], C_HEADS)[None, :],
        wg=w_gate[l].astype(BF16), wba=w_br_a[l].astype(BF16), wbb=w_br_b[l].astype(BF16),
        wbc=w_br_c[l].astype(BF16), wo=w_out[l].astype(BF16), n2=norm2[l][None, :],
        wfi=w_ffn_in[l].astype(BF16), wfo=w_ffn_out[l].astype(BF16))


def _layer(x2, p, tabs, batch, seq):
    t = batch * seq
    a0, a1, a2, qkv_b, qkv_c, z, ba = _inproj(x2, p["n1"], p["w1"], *tabs, p["gains"], p["conv_w"], batch, seq)
    a_res = _mixer_a((a0.reshape(batch, 1, seq, A_QKV), a1, a2))
    o_b = _mixer_b(qkv_b, p["sink"], batch, seq)
    factors = _delta_chunks(qkv_c.reshape(batch, seq, K1_C), ba.reshape(batch, seq, K1_BA), p["alog"], p["dtb"])
    o_cf, o_cb = _delta_scan(factors)
    x2 = _merge(x2, p["n1"], a_res, o_b, o_cf.reshape(t, C_WIDTH), o_cb.reshape(t, C_WIDTH), z, p["ogain"],
                p["wg"], p["wba"], p["wbb"], p["wbc"], p["wo"], seq)
    return _ffn(x2, p["n2"], p["wfi"], p["wfo"])


def kernel(x_prompt, x_sample, norm1, w_in, qk_gain, sink, conv_w, a_log, dt_bias, o_gain, w_gate, w_br_a, w_br_b,
           w_br_c, w_out, norm2, w_ffn_in, w_ffn_out):
    depth = w_in.shape[0]
    layers = [_pack_layer(l, norm1, w_in, qk_gain, sink, conv_w, a_log, dt_bias, o_gain, w_gate, w_br_a, w_br_b,
                          w_br_c, w_out, norm2, w_ffn_in, w_ffn_out) for l in range(depth)]
    outs = []
    for x in (x_prompt, x_sample):
        batch, seq, _ = x.shape
        assert seq % (A_DILATIONS[-1] * 256) == 0
        tabs = _rope_tables(seq)
        x2 = x.reshape(batch * seq, D_MODEL)
        for p in layers:
            x2 = _layer(x2, p, tabs, batch, seq)
        outs.append(x2.reshape(batch, seq, D_MODEL))
    return tuple(outs)
```
